```python
import jax, jax.numpy as jnp
from jax import lax
import numpy as np

D_MODEL = 1024
BATCH = 32
SEQ = 256
DEPTH = 4
DEC_BATCH = 4
DEC_SEQ = 4096
PAST_LEN = 256

GRID_W = 64
HEAD_DIM = 64
N_MIX_HEADS = D_MODEL // HEAD_DIM
CONV_GROUPS = N_MIX_HEADS // 4
GLA_HEADS = (N_MIX_HEADS - CONV_GROUPS) // 2
NAT_HEADS = N_MIX_HEADS - CONV_GROUPS - GLA_HEADS
CONV_W = CONV_GROUPS * HEAD_DIM
CONV_K = 3
GLA_HK = HEAD_DIM // 2
GLA_HV = HEAD_DIM
GLA_LR = 16
GLA_GATE_NORM = 16.0
GLA_CHUNK = 64
NAT_KH = 8
NAT_KW = 16
ROPE_BASE = 10000.0
D_FF = ((8 * D_MODEL // 3 + 127) // 128) * 128
N_EXPERTS = 8
TOP_K = 2
D_EXPERT = 7 * D_MODEL // 2
N_DENSE = (DEPTH + 1) // 2
N_MOE = DEPTH // 2
Q_BLOCK = 128
NORM_EPS = 1e-6
NEG_INF = -1e30
IN_SIZES = (CONV_W, CONV_W, CONV_W,
            GLA_HEADS * GLA_HK, GLA_HEADS * GLA_HK, GLA_HEADS * GLA_HV, GLA_HEADS * GLA_HV, 2 * GLA_LR,
            NAT_HEADS * HEAD_DIM, NAT_HEADS * HEAD_DIM, NAT_HEADS * HEAD_DIM)
IN_W = int(sum(IN_SIZES))
IN_SPLITS = tuple(int(s) for s in np.cumsum(IN_SIZES)[:-1])

kernel_name = 'hybrid_diffusion_trunk_step'


def rmsnorm(x, g):
    xf = x.astype(jnp.float32)
    y = xf * lax.rsqrt(jnp.mean(xf * xf, axis=-1, keepdims=True) + NORM_EPS)
    return (y * g.astype(jnp.float32)).astype(x.dtype)


def to_heads(a, n):
    b, t, _ = a.shape
    return a.reshape(b, t, n, -1).transpose(0, 2, 1, 3)


def from_heads(a):
    b, n, t, d = a.shape
    return a.transpose(0, 2, 1, 3).reshape(b, t, n * d)


def modulation(cvec, w, b):
    m = jax.nn.silu(cvec) @ w + b
    return [a[:, None, :] for a in jnp.split(m, 6, axis=-1)]


def short_conv(bg, cg, hv, w):
    u = cg * hv
    t = u.shape[1]
    up = jnp.pad(u, ((0, 0), (CONV_K // 2, CONV_K // 2), (0, 0)))
    return bg * sum(w[j] * up[:, j:j + t] for j in range(CONV_K))


def axial_rope(x):
    t = jnp.arange(x.shape[2])
    half = x.shape[-1] // 2
    nf = half // 2
    freqs = ROPE_BASE ** (-jnp.arange(nf, dtype=jnp.float32) / nf)

    def rot(xa, pos):
        ang = pos.astype(jnp.float32)[:, None] * freqs
        cos, sin = jnp.cos(ang).astype(x.dtype), jnp.sin(ang).astype(x.dtype)
        a, b = xa[..., :nf], xa[..., nf:]
        return jnp.concatenate([a * cos - b * sin, a * sin + b * cos], axis=-1)

    return jnp.concatenate([rot(x[..., :half], t // GRID_W), rot(x[..., half:], t % GRID_W)], axis=-1)


def gla_chunked(q, k, v, g, s0):
    b_, h_, t_, _ = q.shape
    n = t_ // GLA_CHUNK

    def blocks(a):
        return a.astype(jnp.float32).reshape(b_, h_, n, GLA_CHUNK, a.shape[-1])

    q, k, v, g = blocks(q), blocks(k), blocks(v), blocks(g)
    cum = jnp.cumsum(g, axis=3)
    cum_end = cum[:, :, :, -1:]
    qd = q * jnp.exp(cum)
    kd = k * jnp.exp(-cum)
    kst = k * jnp.exp(cum_end - cum)
    lower = jnp.tril(jnp.ones((GLA_CHUNK, GLA_CHUNK), dtype=bool))
    a = jnp.where(lower, jnp.einsum('bhnik,bhnjk->bhnij', qd, kd), 0.0)
    o_intra = jnp.einsum('bhnij,bhnjv->bhniv', a, v)
    u = jnp.einsum('bhnck,bhncv->bhnkv', kst, v)
    decay = jnp.exp(cum_end[:, :, :, 0])

    def step(s, inp):
        d, u_n = inp
        return d[..., None] * s + u_n, s

    s_fin, s_prev = lax.scan(step, s0.astype(jnp.float32),
                             (jnp.moveaxis(decay, 2, 0), jnp.moveaxis(u, 2, 0)))
    o_inter = jnp.einsum('bhnck,nbhkv->bhncv', qd, s_prev)
    return (o_intra + o_inter).reshape(b_, h_, t_, -1), s_fin


def gla_bidir(q, k, v, g_f, g_b, s0_f, s0_b):
    o_f, s_f = gla_chunked(q, k, v, g_f, s0_f)
    flip = lambda a: jnp.flip(a, axis=2)
    o_b, s_b = gla_chunked(flip(q), flip(k), flip(v), flip(g_b), s0_b)
    return o_f + flip(o_b), s_f, s_b


def gla_gates(lr, w2, b):
    lr_f, lr_b = jnp.split(lr, 2, axis=-1)

    def gate(z, d):
        return to_heads(jax.nn.log_sigmoid((z @ w2[d] + b[d]).astype(jnp.float32)) / GLA_GATE_NORM, GLA_HEADS)

    return gate(lr_f, 0), gate(lr_b, 1)


def gla_readout(o, r, norm_g):
    return from_heads(rmsnorm(o, norm_g)).astype(r.dtype) * jax.nn.silu(r)


def ctx_attention(q, k, v):
    b_, h_, l_, dh = q.shape
    qb = jnp.moveaxis(q.reshape(b_, h_, l_ // Q_BLOCK, Q_BLOCK, dh), 2, 0)

    def blk(qi):
        s = jnp.einsum('bhqd,bhkd->bhqk', qi, k).astype(jnp.float32) * (HEAD_DIM ** -0.5)
        p = jax.nn.softmax(s, axis=-1).astype(v.dtype)
        return jnp.einsum('bhqk,bhkd->bhqd', p, v)

    o = lax.map(blk, qb)
    return jnp.moveaxis(o, 0, 2).reshape(b_, h_, l_, dh)


def neighbourhood_attention(q, k, v, ctx_k, ctx_v, rpb):
    b_, h_, t_, dh = q.shape
    rows = t_ // GRID_W
    kh = min(NAT_KH, rows)
    qg = q.reshape(b_, h_, rows, GRID_W, dh)
    kg = k.reshape(b_, h_, rows, GRID_W, dh)
    vg = v.reshape(b_, h_, rows, GRID_W, dh)
    r = jnp.arange(rows)
    row_idx = jnp.clip(r - kh // 2, 0, rows - kh)[:, None] + jnp.arange(kh)[None, :]
    dr = row_idx - r[:, None]
    col = jnp.arange(GRID_W)
    c0 = jnp.clip(col - NAT_KW // 2, 0, GRID_W - NAT_KW)
    in_win = (col[None, :] >= c0[:, None]) & (col[None, :] < c0[:, None] + NAT_KW)
    dc = jnp.clip(col[None, :] - col[:, None], -(NAT_KW - 1), NAT_KW - 1) + NAT_KW - 1
    scale = HEAD_DIM ** -0.5
    n_loc = kh * GRID_W

    def row_block(xs):
        q_r, idx, dr_r = xs
        k_r = kg[:, :, idx]
        v_r = vg[:, :, idx]
        bias = rpb[:, (dr_r + NAT_KH - 1)[None, :, None], dc[:, None, :]]
        s_loc = jnp.einsum('bhcd,bhkwd->bhckw', q_r, k_r).astype(jnp.float32) * scale + bias.astype(jnp.float32)
        s_loc = jnp.where(in_win[:, None, :], s_loc, NEG_INF)
        s_ctx = jnp.einsum('bhcd,bhld->bhcl', q_r, ctx_k).astype(jnp.float32) * scale
        s = jnp.concatenate([s_loc.reshape(b_, h_, GRID_W, n_loc), s_ctx], axis=-1)
        p = jax.nn.softmax(s, axis=-1).astype(v.dtype)
        p_loc = p[..., :n_loc].reshape(b_, h_, GRID_W, kh, GRID_W)
        return (jnp.einsum('bhckw,bhkwd->bhcd', p_loc, v_r)
                + jnp.einsum('bhcl,bhld->bhcd', p[..., n_loc:], ctx_v))

    o = lax.map(row_block, (jnp.moveaxis(qg, 2, 0), row_idx, dr))
    return jnp.moveaxis(o, 0, 2).reshape(b_, h_, t_, dh)


def context_mixer(h, w_in, conv_w, gla_w2, gla_b, gla_ng, q_ng, k_ng, w_out):
    bg, cg, hv, gq, gk, gv, gr, glr, nq, nk, nv = jnp.split(h @ w_in, IN_SPLITS, axis=-1)
    y_conv = short_conv(bg, cg, hv, conv_w)
    q = to_heads(gq, GLA_HEADS) * (GLA_HK ** -0.5)
    k = to_heads(gk, GLA_HEADS)
    v = to_heads(gv, GLA_HEADS)
    g_f, g_b = gla_gates(glr, gla_w2, gla_b)
    zero = jnp.zeros((h.shape[0], GLA_HEADS, GLA_HK, GLA_HV), jnp.float32)
    o, s_f, s_b = gla_bidir(q, k, v, g_f, g_b, zero, zero)
    y_gla = gla_readout(o, gr, gla_ng)
    qn = rmsnorm(to_heads(nq, NAT_HEADS), q_ng)
    kn = rmsnorm(to_heads(nk, NAT_HEADS), k_ng)
    vn = to_heads(nv, NAT_HEADS)
    y_nat = from_heads(ctx_attention(qn, kn, vn))
    y = jnp.concatenate([y_conv, y_gla, y_nat], axis=-1) @ w_out
    return y, jnp.stack([s_f, s_b], axis=1).astype(h.dtype), kn, vn


def latent_mixer(h, w_in, conv_w, gla_w2, gla_b, gla_ng, q_ng, k_ng, w_out, rpb, s0_f, s0_b, ctx_k, ctx_v):
    bg, cg, hv, gq, gk, gv, gr, glr, nq, nk, nv = jnp.split(h @ w_in, IN_SPLITS, axis=-1)
    y_conv = short_conv(bg, cg, hv, conv_w)
    q = axial_rope(to_heads(gq, GLA_HEADS) * (GLA_HK ** -0.5))
    k = axial_rope(to_heads(gk, GLA_HEADS))
    v = to_heads(gv, GLA_HEADS)
    g_f, g_b = gla_gates(glr, gla_w2, gla_b)
    o, _, _ = gla_bidir(q, k, v, g_f, g_b, s0_f, s0_b)
    y_gla = gla_readout(o, gr, gla_ng)
    qn = rmsnorm(to_heads(nq, NAT_HEADS), q_ng)
    kn = rmsnorm(to_heads(nk, NAT_HEADS), k_ng)
    vn = to_heads(nv, NAT_HEADS)
    y_nat = from_heads(neighbourhood_attention(qn, kn, vn, ctx_k, ctx_v, rpb))
    return jnp.concatenate([y_conv, y_gla, y_nat], axis=-1) @ w_out


def swiglu(h, wg, wu, wd):
    return (jax.nn.silu(h @ wg) * (h @ wu)) @ wd


def moe_ffn(h, router, wg, wu, wd):
    b_, t_, d_ = h.shape
    tok = h.reshape(b_ * t_, d_)
    logits = (tok @ router).astype(jnp.float32)
    top_v, top_i = lax.top_k(logits, TOP_K)
    top_w = jax.nn.softmax(top_v, axis=-1)
    combine = jnp.sum(jax.nn.one_hot(top_i, N_EXPERTS, dtype=jnp.float32) * top_w[..., None], axis=1).astype(h.dtype)
    out = jnp.zeros_like(tok)
    for e in range(N_EXPERTS):
        out = out + combine[:, e:e + 1] * swiglu(tok, wg[e], wu[e], wd[e])
    return out.reshape(b_, t_, d_)


def channel_mixer(h, i, ffn_w_gate, ffn_w_up, ffn_w_down, moe_router, moe_w_gate, moe_w_up, moe_w_down):
    j = i // 2
    if i % 2 == 0:
        return swiglu(h, ffn_w_gate[j], ffn_w_up[j], ffn_w_down[j])
    return moe_ffn(h, moe_router[j], moe_w_gate[j], moe_w_up[j], moe_w_down[j])


def setup_inputs(seed: int = 0) -> dict:
    key = jax.random.key(seed)
    keys = iter(jax.random.split(key, 40))
    D = D_MODEL

    def nrm(shape, scale):
        return jax.random.normal(next(keys), shape, jnp.float32) * scale

    def gain(shape):
        return 1.0 + nrm(shape, 0.02)

    return {
        'x_prompt': nrm((BATCH, SEQ, D), 1.0),
        'x_sample': nrm((DEC_BATCH, DEC_SEQ, D), 1.0),
        'cache_nat_k': nrm((DEC_BATCH, DEPTH, NAT_HEADS, PAST_LEN, HEAD_DIM), 1.0),
        'cache_nat_v': nrm((DEC_BATCH, DEPTH, NAT_HEADS, PAST_LEN, HEAD_DIM), 1.0),
        'state_gla': nrm((DEC_BATCH, DEPTH, 2, GLA_HEADS, GLA_HK, GLA_HV), 1.0),
        'c': nrm((DEC_BATCH, D), 1.0),
        'c_ctx': nrm((D,), 1.0),
        'norm1_g': gain((DEPTH, D)),
        'norm2_g': gain((DEPTH, D)),
        'w_mod': nrm((DEPTH, D, 6 * D), 0.5 * D ** -0.5),
        'b_mod': nrm((DEPTH, 6 * D), 0.02),
        'w_in': nrm((DEPTH, D, IN_W), D ** -0.5),
        'conv_w': nrm((DEPTH, CONV_K, CONV_W), CONV_K ** -0.5),
        'gla_gate_w2': nrm((DEPTH, 2, GLA_LR, GLA_HEADS * GLA_HK), GLA_LR ** -0.5),
        'gla_gate_b': nrm((DEPTH, 2, GLA_HEADS * GLA_HK), 0.1),
        'gla_norm_g': gain((DEPTH, GLA_HV)),
        'nat_q_norm_g': gain((DEPTH, HEAD_DIM)),
        'nat_k_norm_g': gain((DEPTH, HEAD_DIM)),
        'nat_rpb': nrm((DEPTH, NAT_HEADS, 2 * NAT_KH - 1, 2 * NAT_KW - 1), 0.1),
        'w_out': nrm((DEPTH, D, D), D ** -0.5),
        'ffn_w_gate': nrm((N_DENSE, D, D_FF), D ** -0.5),
        'ffn_w_up': nrm((N_DENSE, D, D_FF), D ** -0.5),
        'ffn_w_down': nrm((N_DENSE, D_FF, D), D_FF ** -0.5),
        'moe_router': nrm((N_MOE, D, N_EXPERTS), D ** -0.5),
        'moe_w_gate': nrm((N_MOE, N_EXPERTS, D, D_EXPERT), D ** -0.5),
        'moe_w_up': nrm((N_MOE, N_EXPERTS, D, D_EXPERT), D ** -0.5),
        'moe_w_down': nrm((N_MOE, N_EXPERTS, D_EXPERT, D), D_EXPERT ** -0.5),
    }


def reference(x_prompt, x_sample, cache_nat_k, cache_nat_v, state_gla, c, c_ctx,
              norm1_g, norm2_g, w_mod, b_mod, w_in, conv_w, gla_gate_w2, gla_gate_b, gla_norm_g,
              nat_q_norm_g, nat_k_norm_g, nat_rpb, w_out, ffn_w_gate, ffn_w_up, ffn_w_down,
              moe_router, moe_w_gate, moe_w_up, moe_w_down):
    xp, xs = x_prompt, x_sample
    new_k, new_v, new_s = [], [], []
    for i in range(DEPTH):
        lw = (w_in[i], conv_w[i], gla_gate_w2[i], gla_gate_b[i], gla_norm_g[i],
              nat_q_norm_g[i], nat_k_norm_g[i], w_out[i])
        ffw = (ffn_w_gate, ffn_w_up, ffn_w_down, moe_router, moe_w_gate, moe_w_up, moe_w_down)
        sh1, sc1, g1, sh2, sc2, g2 = modulation(c_ctx[None, :], w_mod[i], b_mod[i])
        y, st, kn, vn = context_mixer(rmsnorm(xp, norm1_g[i]) * (1 + sc1) + sh1, *lw)
        xp = xp + g1 * y
        xp = xp + g2 * channel_mixer(rmsnorm(xp, norm2_g[i]) * (1 + sc2) + sh2, i, *ffw)
        new_k.append(kn)
        new_v.append(vn)
        new_s.append(st)
        sh1, sc1, g1, sh2, sc2, g2 = modulation(c, w_mod[i], b_mod[i])
        y = latent_mixer(rmsnorm(xs, norm1_g[i]) * (1 + sc1) + sh1, *lw, nat_rpb[i],
                         state_gla[:, i, 0], state_gla[:, i, 1], cache_nat_k[:, i], cache_nat_v[:, i])
        xs = xs + g1 * y
        xs = xs + g2 * channel_mixer(rmsnorm(xs, norm2_g[i]) * (1 + sc2) + sh2, i, *ffw)
    return (xp, xs, jnp.stack(new_k, axis=1), jnp.stack(new_v, axis=1), jnp.stack(new_s, axis=1))
```

```python
import functools

import numpy as np
import jax
import jax.numpy as jnp
from jax import lax
from jax.experimental import pallas as pl
from jax.experimental.pallas import tpu as pltpu

F32 = jnp.float32
BF16 = jnp.bfloat16

D = 1024
SEQ = 256
TLAT = 4096
GRID_W = 64
HD = 64
CONV_W = 256
GH, GK, GV = 6, 32, 64
NH = 6
LR = 16
GATE_NORM = 16.0
CHUNK = 64
NAT_KH, NAT_KW = 8, 16
ROPE_BASE = 10000.0
D_FF = 2816
N_EXP = 8
D_EXP = 3584
EPS = 1e-6
NEG = -1e30

GW = GH * GK
GWP = 256
GVW = GH * GV
NW = NH * HD

PG_W = 640
PC_W = 768
PV_W = 768
PN_W = 1152
W_IN_P = PG_W + PC_W + PV_W + PN_W

TM_PROJ = 512
BLK = 256
TQ = 512
BAND = 1024
TM_FFN = 1024
TF_FFN = 256
TM_MOE = 512
TF_MOE = 512
TC_COMB = 256

VMEM_LIMIT = 56 * 1024 * 1024


def _cparams(sem):
    return pltpu.CompilerParams(dimension_semantics=sem, vmem_limit_bytes=VMEM_LIMIT)


def _dot(a, b):
    return jnp.dot(a, b, preferred_element_type=F32)


def _dot_nt(a, b):
    return lax.dot_general(a, b, (((1,), (1,)), ((), ())), preferred_element_type=F32)


def _dot_tn(a, b):
    return lax.dot_general(a, b, (((0,), (0,)), ((), ())), preferred_element_type=F32)


def _split_bf16(a):
    hi = a.astype(BF16)
    lo = (a - hi.astype(F32)).astype(BF16)
    return hi, lo


def _silu(a):
    return a * jax.nn.sigmoid(a)


def _head_mean(sq, hm):
    hi, lo = _split_bf16(sq)
    return _dot(hi, hm) + _dot(lo, hm)


def _mod_kernel(c_ref, w_ref, b_ref, o_ref):
    s = _silu(c_ref[...])
    o_ref[...] = _dot(s.astype(BF16), w_ref[...].astype(BF16)) + b_ref[...]


def _modulation(cvecs, w_mod, b_mod):
    nl = w_mod.shape[0]
    out = pl.pallas_call(
        _mod_kernel,
        grid=(nl, 6),
        in_specs=[
            pl.BlockSpec((8, D), lambda l, j: (0, 0)),
            pl.BlockSpec((None, D, D), lambda l, j: (l, 0, j)),
            pl.BlockSpec((None, 1, D), lambda l, j: (l, 0, j)),
        ],
        out_specs=pl.BlockSpec((None, 8, D), lambda l, j: (l, 0, j)),
        out_shape=jax.ShapeDtypeStruct((nl, 8, 6 * D), F32),
        compiler_params=_cparams(("arbitrary", "arbitrary")),
    )(cvecs, w_mod, b_mod.reshape(nl, 1, 6 * D))
    return out.reshape(nl, 8, 6, D)


def _norm_mod(x, g, shift, scale):
    ms = jnp.mean(x * x, axis=-1, keepdims=True)
    return x * lax.rsqrt(ms + EPS) * g * (1.0 + scale) + shift


def _proj_in_kernel(mrow_ref, x_ref, g_ref, mod_ref, w_ref, hm_ref, qg_ref, kg_ref,
                    pg_ref, pc_ref, pv_ref, pn_ref, knf_ref, vnf_ref):
    del mrow_ref
    h = _norm_mod(x_ref[...], g_ref[...], mod_ref[0:1, :], mod_ref[1:2, :]).astype(BF16)
    pg_ref[...] = _dot(h, w_ref[:, 0:PG_W]).astype(BF16)
    pc_ref[...] = _dot(h, w_ref[:, PG_W:PG_W + PC_W]).astype(BF16)
    pv_ref[...] = _dot(h, w_ref[:, PG_W + PC_W:PG_W + PC_W + PV_W]).astype(BF16)
    o = PG_W + PC_W + PV_W
    nq = _dot(h, w_ref[:, o:o + NW])
    nk = _dot(h, w_ref[:, o + NW:o + 2 * NW])
    nv = _dot(h, w_ref[:, o + 2 * NW:o + 3 * NW])
    hm = hm_ref[...]
    qn = nq * lax.rsqrt(_head_mean(nq * nq, hm) + EPS) * qg_ref[...]
    kn = nk * lax.rsqrt(_head_mean(nk * nk, hm) + EPS) * kg_ref[...]
    pn_ref[:, 0:NW] = (qn * (HD ** -0.5)).astype(BF16)
    pn_ref[:, NW:2 * NW] = kn.astype(BF16)
    pn_ref[:, 2 * NW:3 * NW] = nv.astype(BF16)
    knf_ref[...] = kn
    vnf_ref[...] = nv


def _proj_in(x, g1, mod, w_in_p, hm, qg, kg, mrow, n_ctx_rows):
    r = x.shape[0]
    nt = r // TM_PROJ
    nct = n_ctx_rows // TM_PROJ
    row = lambda t, m: (t, 0)
    const = lambda t, m: (0, 0)
    ctx_only = lambda t, m: (jnp.minimum(t, nct), 0)
    grid_spec = pltpu.PrefetchScalarGridSpec(
        num_scalar_prefetch=1,
        grid=(nt,),
        in_specs=[
            pl.BlockSpec((TM_PROJ, D), row),
            pl.BlockSpec((1, D), const),
            pl.BlockSpec((None, 6, D), lambda t, m: (m[t], 0, 0)),
            pl.BlockSpec((D, W_IN_P), const),
            pl.BlockSpec((NW, NW), const),
            pl.BlockSpec((1, NW), const),
            pl.BlockSpec((1, NW), const),
        ],
        out_specs=[
            pl.BlockSpec((TM_PROJ, PG_W), row),
            pl.BlockSpec((TM_PROJ, PC_W), row),
            pl.BlockSpec((TM_PROJ, PV_W), row),
            pl.BlockSpec((TM_PROJ, PN_W), row),
            pl.BlockSpec((TM_PROJ, NW), ctx_only),
            pl.BlockSpec((TM_PROJ, NW), ctx_only),
        ],
    )
    return pl.pallas_call(
        _proj_in_kernel,
        grid_spec=grid_spec,
        out_shape=[
            jax.ShapeDtypeStruct((r, PG_W), BF16),
            jax.ShapeDtypeStruct((r, PC_W), BF16),
            jax.ShapeDtypeStruct((r, PV_W), BF16),
            jax.ShapeDtypeStruct((r, PN_W), BF16),
            jax.ShapeDtypeStruct((n_ctx_rows + TM_PROJ, NW), F32),
            jax.ShapeDtypeStruct((n_ctx_rows + TM_PROJ, NW), F32),
        ],
        compiler_params=_cparams(("arbitrary",)),
    )(mrow, x, g1, mod, w_in_p, hm, qg, kg)


def _log_sigmoid(z):
    return jnp.minimum(z, 0.0) - jnp.log1p(jnp.exp(-jnp.abs(z)))


def _gla_direction(pg, v, cos, sina, sinb, w2, gb, tri, st_ref, o_ref, rev):
    q = pg[:, 0:GWP].astype(F32) * (GK ** -0.5)
    k = pg[:, GWP:2 * GWP].astype(F32)
    q = q * cos + pltpu.roll(q, GWP - 8, 1) * sina + pltpu.roll(q, 8, 1) * sinb
    k = k * cos + pltpu.roll(k, GWP - 8, 1) * sina + pltpu.roll(k, 8, 1) * sinb
    z = _dot(pg[:, 2 * GWP:2 * GWP + 128], w2) + gb
    g = _log_sigmoid(z) * (1.0 / GATE_NORM)
    ghi, glo = _split_bf16(g)
    cum = _dot(tri, ghi) + _dot(tri, glo)
    qd = (q * jnp.exp(cum)).astype(BF16)
    kd = (k * jnp.exp(-cum)).astype(BF16)

    lane_k = lax.broadcasted_iota(jnp.int32, (1, GWP), 1)
    head_masks = [(lane_k >> 5) == h for h in range(GH)]
    row_a = lax.broadcasted_iota(jnp.int32, (CHUNK, GVW), 0)
    col_a = lax.broadcasted_iota(jnp.int32, (CHUNK, GVW), 1) & (CHUNK - 1)
    keep_a = (col_a >= row_a) if rev else (col_a <= row_a)
    row_v = lax.broadcasted_iota(jnp.int32, (GVW, GVW), 0) >> 6
    col_v = lax.broadcasted_iota(jnp.int32, (GVW, GVW), 1) >> 6
    keep_v = row_v == col_v
    row_s = lax.broadcasted_iota(jnp.int32, (GVW, GWP), 0) >> 6
    col_s = lax.broadcasted_iota(jnp.int32, (GVW, GWP), 1) >> 5
    keep_s = row_s == col_s

    chunks = range(BLK // CHUNK)
    for c in (reversed(chunks) if rev else chunks):
        lo = c * CHUNK
        sl = slice(lo, lo + CHUNK)
        edge = lo if rev else lo + CHUNK - 1
        cend = cum[edge:edge + 1, :]
        kst = (k[sl] * jnp.exp(cend - cum[sl])).astype(BF16)
        decay = jnp.exp(cend)
        kd_c = kd[sl]
        kblk = jnp.concatenate(
            [jnp.where(head_masks[h], kd_c, jnp.zeros_like(kd_c)) for h in range(GH)], axis=0)
        a = jnp.where(keep_a, _dot_nt(qd[sl], kblk), 0.0).astype(BF16)
        v_c = v[sl]
        vblk = jnp.where(keep_v, jnp.concatenate([v_c] * GH, axis=0), jnp.zeros((), BF16))
        st = st_ref[...]
        o = _dot(a, vblk) + _dot_nt(qd[sl], st.astype(BF16))
        o_ref[sl, :] = o.astype(o_ref.dtype)
        ut = _dot_tn(v_c, kst)
        st_ref[...] = st * decay + jnp.where(keep_s, ut, 0.0)


def _gla_kernel(fblk, bblk, seq, first, tblf, tblb,
                pgf_ref, pvf_ref, pgb_ref, pvb_ref,
                cosf_ref, sinaf_ref, sinbf_ref, cosb_ref, sinab_ref, sinbb_ref,
                w2f_ref, w2b_ref, gbf_ref, gbb_ref, trif_ref, trib_ref, s0f_ref, s0b_ref,
                of_ref, ob_ref, sff_ref, sfb_ref, stf, stb):
    del fblk, bblk, seq, tblf, tblb
    u = pl.program_id(0)

    @pl.when(first[u] == 1)
    def _():
        stf[...] = s0f_ref[...]
        stb[...] = s0b_ref[...]

    _gla_direction(pgf_ref[...], pvf_ref[...], cosf_ref[...], sinaf_ref[...], sinbf_ref[...],
                   w2f_ref[...], gbf_ref[...], trif_ref[...], stf, of_ref, False)
    _gla_direction(pgb_ref[...], pvb_ref[...], cosb_ref[...], sinab_ref[...], sinbb_ref[...],
                   w2b_ref[...], gbb_ref[...], trib_ref[...], stb, ob_ref, True)
    sff_ref[...] = stf[...]
    sfb_ref[...] = stb[...]


def _gla_plan(n_ctx, n_lat):
    nb = TLAT // BLK
    fblk, bblk, seq, first, tblf, tblb = [], [], [], [], [], []
    for s in range(n_ctx):
        fblk.append(s); bblk.append(s); seq.append(s); first.append(1)
        tblf.append(nb); tblb.append(nb)
    for s in range(n_lat):
        for j in range(nb):
            fblk.append(n_ctx + s * nb + j)
            bblk.append(n_ctx + s * nb + nb - 1 - j)
            seq.append(n_ctx + s)
            first.append(1 if j == 0 else 0)
            tblf.append(j); tblb.append(nb - 1 - j)
    return [np.asarray(a, np.int32) for a in (fblk, bblk, seq, first, tblf, tblb)]


def _gla(pg, pv, rope, w2f, w2b, gbf, gbb, trif, trib, s0f, s0b, plan):
    r = pg.shape[0]
    nseq = s0f.shape[0]
    nsteps = plan[0].shape[0]
    cos, sina, sinb = rope
    fb = lambda u, f, b, s, fi, tf, tb: (f[u], 0)
    bb = lambda u, f, b, s, fi, tf, tb: (b[u], 0)
    tfm = lambda u, f, b, s, fi, tf, tb: (tf[u], 0)
    tbm = lambda u, f, b, s, fi, tf, tb: (tb[u], 0)
    const = lambda u, f, b, s, fi, tf, tb: (0, 0)
    sq = lambda u, f, b, s, fi, tf, tb: (s[u], 0, 0)
    tab = pl.BlockSpec((BLK, GWP), tfm)
    tabb = pl.BlockSpec((BLK, GWP), tbm)
    grid_spec = pltpu.PrefetchScalarGridSpec(
        num_scalar_prefetch=6,
        grid=(nsteps,),
        in_specs=[
            pl.BlockSpec((BLK, PG_W), fb), pl.BlockSpec((BLK, GVW), fb),
            pl.BlockSpec((BLK, PG_W), bb), pl.BlockSpec((BLK, GVW), bb),
            tab, tab, tab, tabb, tabb, tabb,
            pl.BlockSpec((128, GWP), const), pl.BlockSpec((128, GWP), const),
            pl.BlockSpec((1, GWP), const), pl.BlockSpec((1, GWP), const),
            pl.BlockSpec((BLK, BLK), const), pl.BlockSpec((BLK, BLK), const),
            pl.BlockSpec((None, GVW, GWP), sq), pl.BlockSpec((None, GVW, GWP), sq),
        ],
        out_specs=[
            pl.BlockSpec((BLK, GVW), fb), pl.BlockSpec((BLK, GVW), bb),
            pl.BlockSpec((None, GVW, GWP), sq), pl.BlockSpec((None, GVW, GWP), sq),
        ],
        scratch_shapes=[pltpu.VMEM((GVW, GWP), F32), pltpu.VMEM((GVW, GWP), F32)],
    )
    return pl.pallas_call(
        _gla_kernel,
        grid_spec=grid_spec,
        out_shape=[
            jax.ShapeDtypeStruct((r, GVW), BF16), jax.ShapeDtypeStruct((r, GVW), BF16),
            jax.ShapeDtypeStruct((nseq, GVW, GWP), F32), jax.ShapeDtypeStruct((nseq, GVW, GWP), F32),
        ],
        compiler_params=_cparams(("arbitrary",)),
    )(*plan, pg, pv, pg, pv, cos, sina, sinb, cos, sina, sinb,
      w2f, w2b, gbf, gbb, trif, trib, s0f, s0b)


def _pair_attention(q, keys, vals, biases):
    lane = lax.broadcasted_iota(jnp.int32, (1, 2 * HD), 1)
    first = lane < HD
    outs = []
    for half in range(2):
        qm = jnp.where(first if half == 0 else jnp.logical_not(first), q, jnp.zeros_like(q))
        ss = []
        for kk, bias in zip(keys, biases):
            s = _dot_nt(qm, kk)
            if bias is not None:
                s = s + bias[half]
            ss.append(s)
        m = ss[0].max(axis=-1, keepdims=True)
        for s in ss[1:]:
            m = jnp.maximum(m, s.max(axis=-1, keepdims=True))
        acc = None
        den = None
        for s, vv in zip(ss, vals):
            e = jnp.exp(s - m)
            d = e.sum(axis=-1, keepdims=True)
            o = _dot(e.astype(BF16), vv)
            acc = o if acc is None else acc + o
            den = d if den is None else den + d
        outs.append(acc / den)
    return jnp.where(first, outs[0], outs[1])


def _ctx_attn_kernel(q_ref, k_ref, v_ref, o_ref):
    for p in range(NH // 2):
        sl = slice(p * 2 * HD, (p + 1) * 2 * HD)
        o = _pair_attention(q_ref[:, sl], [k_ref[:, sl]], [v_ref[:, sl]], [None])
        o_ref[:, sl] = o.astype(o_ref.dtype)


def _ctx_attn(pn, n_ctx):
    return pl.pallas_call(
        _ctx_attn_kernel,
        grid=(n_ctx,),
        in_specs=[
            pl.BlockSpec((SEQ, NW), lambda b: (b, 0)),
            pl.BlockSpec((SEQ, NW), lambda b: (b, 1)),
            pl.BlockSpec((SEQ, NW), lambda b: (b, 2)),
        ],
        out_specs=pl.BlockSpec((SEQ, NW), lambda b: (b, 0)),
        out_shape=jax.ShapeDtypeStruct((n_ctx * SEQ, NW), BF16),
        compiler_params=_cparams(("arbitrary",)),
    )(pn, pn, pn)


def _nat_kernel(q_ref, k_ref, v_ref, ck_ref, cv_ref, bias_ref, o_ref):
    j = pl.program_id(2)
    start = pl.multiple_of(jnp.clip(j * TQ - BAND // 4, 0, TLAT - BAND), 256)
    kb = k_ref[pl.ds(start, BAND), :]
    vb = v_ref[pl.ds(start, BAND), :]
    o = _pair_attention(q_ref[...], [kb, ck_ref[...]], [vb, cv_ref[...]],
                        [(bias_ref[0], bias_ref[1]), None])
    o_ref[...] = o.astype(o_ref.dtype)


def _nat_attn(pn, ck, cv, bias, n_ctx_rows, n_lat):
    nj = TLAT // TQ
    qb0 = n_ctx_rows // TQ
    sb0 = n_ctx_rows // TLAT
    npair = NH // 2
    btype = lambda j: jnp.where(j == 0, 0, jnp.where(j == nj - 1, 2, 1))
    return pl.pallas_call(
        _nat_kernel,
        grid=(n_lat, npair, nj),
        in_specs=[
            pl.BlockSpec((TQ, 2 * HD), lambda b, p, j: (qb0 + b * nj + j, p)),
            pl.BlockSpec((TLAT, 2 * HD), lambda b, p, j: (sb0 + b, npair + p)),
            pl.BlockSpec((TLAT, 2 * HD), lambda b, p, j: (sb0 + b, 2 * npair + p)),
            pl.BlockSpec((None, SEQ, 2 * HD), lambda b, p, j: (b, 0, p)),
            pl.BlockSpec((None, SEQ, 2 * HD), lambda b, p, j: (b, 0, p)),
            pl.BlockSpec((None, 2, TQ, BAND), lambda b, p, j: (btype(j), p, 0, 0)),
        ],
        out_specs=pl.BlockSpec((TQ, 2 * HD), lambda b, p, j: (b * nj + j, p)),
        out_shape=jax.ShapeDtypeStruct((n_lat * TLAT, NW), BF16),
        compiler_params=_cparams(("arbitrary", "arbitrary", "arbitrary")),
    )(pn, pn, pn, ck, cv, bias)


def _proj_out_kernel(mrow, isctx, cidx, lidx, hprev, hnext, pblk, nblk,
                     x_ref, mod_ref, pc_ref, pcp_ref, pcn_ref, cw_ref, of_ref, ob_ref, gr_ref,
                     gng_ref, hm_ref, yc_ref, yl_ref, w_ref, o_ref):
    del mrow, cidx, lidx, pblk, nblk
    t = pl.program_id(0)
    pc = pc_ref[...].astype(F32)
    u = pc[:, CONV_W:2 * CONV_W] * pc[:, 2 * CONV_W:3 * CONV_W]
    pp = pcp_ref[7:8, :].astype(F32)
    pn = pcn_ref[0:1, :].astype(F32)
    u_prev_edge = pp[:, CONV_W:2 * CONV_W] * pp[:, 2 * CONV_W:3 * CONV_W] * hprev[t].astype(F32)
    u_next_edge = pn[:, CONV_W:2 * CONV_W] * pn[:, 2 * CONV_W:3 * CONV_W] * hnext[t].astype(F32)
    rows = lax.broadcasted_iota(jnp.int32, (BLK, CONV_W), 0)
    u_prev = jnp.where(rows == 0, u_prev_edge, pltpu.roll(u, 1, 0))
    u_next = jnp.where(rows == BLK - 1, u_next_edge, pltpu.roll(u, BLK - 1, 0))
    y_conv = pc[:, 0:CONV_W] * (cw_ref[0:1, :] * u_prev + cw_ref[1:2, :] * u + cw_ref[2:3, :] * u_next)

    o = of_ref[...].astype(F32) + ob_ref[...].astype(F32)
    on = o * lax.rsqrt(_head_mean(o * o, hm_ref[...]) + EPS) * gng_ref[...]
    y_gla = on * _silu(gr_ref[...].astype(F32))

    y_nat = jnp.where(isctx[t] == 1, yc_ref[...], yl_ref[...])

    y = (_dot(y_conv.astype(BF16), w_ref[0:CONV_W, :])
         + _dot(y_gla.astype(BF16), w_ref[CONV_W:CONV_W + GVW, :])
         + _dot(y_nat, w_ref[CONV_W + GVW:D, :]))
    o_ref[...] = x_ref[...] + mod_ref[2:3, :] * y


def _proj_out_plan(n_ctx, n_lat):
    nb = TLAT // BLK
    nt = n_ctx + n_lat * nb
    mrow, isctx, cidx, lidx, hprev, hnext, pblk, nblk = ([] for _ in range(8))
    per = BLK // 8
    for t in range(nt):
        ctx = t < n_ctx
        tl = t - n_ctx
        mrow.append(0 if ctx else 1 + tl // nb)
        isctx.append(1 if ctx else 0)
        cidx.append(min(t, n_ctx - 1))
        lidx.append(max(tl, 0))
        hprev.append(0 if ctx or tl % nb == 0 else 1)
        hnext.append(0 if ctx or tl % nb == nb - 1 else 1)
        pblk.append(max(t * per - 1, 0))
        nblk.append(min((t + 1) * per, nt * per - 1))
    return [np.asarray(a, np.int32) for a in (mrow, isctx, cidx, lidx, hprev, hnext, pblk, nblk)]


def _proj_out(x, mod, pc, conv_w, o_f, o_b, pv, gng, hm, y_ctx, y_lat, w_out, plan):
    r = x.shape[0]
    nt = plan[0].shape[0]
    row = lambda t, *_: (t, 0)
    const = lambda t, *_: (0, 0)
    grid_spec = pltpu.PrefetchScalarGridSpec(
        num_scalar_prefetch=8,
        grid=(nt,),
        in_specs=[
            pl.BlockSpec((BLK, D), row),
            pl.BlockSpec((None, 6, D), lambda t, m, *_: (m[t], 0, 0)),
            pl.BlockSpec((BLK, PC_W), row),
            pl.BlockSpec((8, PC_W), lambda t, m, ic, ci, li, hp, hn, pb, nb_: (pb[t], 0)),
            pl.BlockSpec((8, PC_W), lambda t, m, ic, ci, li, hp, hn, pb, nb_: (nb_[t], 0)),
            pl.BlockSpec((3, CONV_W), const),
            pl.BlockSpec((BLK, GVW), row),
            pl.BlockSpec((BLK, GVW), row),
            pl.BlockSpec((BLK, GVW), lambda t, *_: (t, 1)),
            pl.BlockSpec((1, GVW), const),
            pl.BlockSpec((GVW, GVW), const),
            pl.BlockSpec((BLK, NW), lambda t, m, ic, ci, *_: (ci[t], 0)),
            pl.BlockSpec((BLK, NW), lambda t, m, ic, ci, li, *_: (li[t], 0)),
            pl.BlockSpec((D, D), const),
        ],
        out_specs=pl.BlockSpec((BLK, D), row),
    )
    return pl.pallas_call(
        _proj_out_kernel,
        grid_spec=grid_spec,
        out_shape=jax.ShapeDtypeStruct((r, D), F32),
        compiler_params=_cparams(("arbitrary",)),
    )(*plan, x, mod, pc, pc, pc, conv_w, o_f, o_b, pv, gng, hm, y_ctx, y_lat, w_out)


def _ffn_dense_kernel(mrow, x_ref, g_ref, mod_ref, wg_ref, wu_ref, wd_ref, o_ref, h_scr, acc):
    del mrow
    f = pl.program_id(1)

    @pl.when(f == 0)
    def _():
        h_scr[...] = _norm_mod(x_ref[...], g_ref[...], mod_ref[3:4, :], mod_ref[4:5, :]).astype(BF16)
        acc[...] = jnp.zeros_like(acc)

    h = h_scr[...]
    hid = _silu(_dot(h, wg_ref[...])) * _dot(h, wu_ref[...])
    acc[...] += _dot(hid.astype(BF16), wd_ref[...])

    @pl.when(f == pl.num_programs(1) - 1)
    def _():
        o_ref[...] = x_ref[...] + mod_ref[5:6, :] * acc[...]


def _ffn_dense(x, g2, mod, wg, wu, wd, mrow):
    r = x.shape[0]
    nf = D_FF // TF_FFN
    grid_spec = pltpu.PrefetchScalarGridSpec(
        num_scalar_prefetch=1,
        grid=(r // TM_FFN, nf),
        in_specs=[
            pl.BlockSpec((TM_FFN, D), lambda t, f, m: (t, 0)),
            pl.BlockSpec((1, D), lambda t, f, m: (0, 0)),
            pl.BlockSpec((None, 6, D), lambda t, f, m: (m[t], 0, 0)),
            pl.BlockSpec((D, TF_FFN), lambda t, f, m: (0, f)),
            pl.BlockSpec((D, TF_FFN), lambda t, f, m: (0, f)),
            pl.BlockSpec((TF_FFN, D), lambda t, f, m: (f, 0)),
        ],
        out_specs=pl.BlockSpec((TM_FFN, D), lambda t, f, m: (t, 0)),
        scratch_shapes=[pltpu.VMEM((TM_FFN, D), BF16), pltpu.VMEM((TM_FFN, D), F32)],
    )
    return pl.pallas_call(
        _ffn_dense_kernel,
        grid_spec=grid_spec,
        out_shape=jax.ShapeDtypeStruct((r, D), F32),
        compiler_params=_cparams(("arbitrary", "arbitrary")),
    )(mrow, x, g2, mod, wg, wu, wd)


def _router_kernel(mrow, x_ref, g_ref, mod_ref, rhi_ref, rlo_ref, h_ref, ri_ref, rw_ref):
    del mrow
    h = _norm_mod(x_ref[...], g_ref[...], mod_ref[3:4, :], mod_ref[4:5, :])
    h_ref[...] = h
    hhi, hlo = _split_bf16(h)
    logits = _dot(hhi, rhi_ref[...]) + _dot(hlo, rhi_ref[...]) + _dot(hhi, rlo_ref[...])
    lane = lax.broadcasted_iota(jnp.int32, logits.shape, 1).astype(F32)
    lg = jnp.where(lane < N_EXP, logits, -jnp.inf)
    m1 = lg.max(axis=-1, keepdims=True)
    i1 = jnp.where(lg == m1, lane, 128.0).min(axis=-1, keepdims=True)
    lg2 = jnp.where(lane == i1, -jnp.inf, lg)
    m2 = lg2.max(axis=-1, keepdims=True)
    i2 = jnp.where(lg2 == m2, lane, 128.0).min(axis=-1, keepdims=True)
    e = jnp.exp(m2 - m1)
    w1 = 1.0 / (1.0 + e)
    w2 = e / (1.0 + e)
    ri_ref[...] = jnp.where(lane == 0.0, i1, jnp.where(lane == 1.0, i2, 0.0)).astype(jnp.int32)
    rw_ref[...] = jnp.where(lane == 0.0, w1, jnp.where(lane == 1.0, w2, 0.0))


def _router(x, g2, mod, rhi, rlo, mrow):
    r = x.shape[0]
    row = lambda t, m: (t, 0)
    const = lambda t, m: (0, 0)
    grid_spec = pltpu.PrefetchScalarGridSpec(
        num_scalar_prefetch=1,
        grid=(r // TM_PROJ,),
        in_specs=[
            pl.BlockSpec((TM_PROJ, D), row),
            pl.BlockSpec((1, D), const),
            pl.BlockSpec((None, 6, D), lambda t, m: (m[t], 0, 0)),
            pl.BlockSpec((D, 128), const),
            pl.BlockSpec((D, 128), const),
        ],
        out_specs=[
            pl.BlockSpec((TM_PROJ, D), row),
            pl.BlockSpec((TM_PROJ, 128), row),
            pl.BlockSpec((TM_PROJ, 128), row),
        ],
    )
    return pl.pallas_call(
        _router_kernel,
        grid_spec=grid_spec,
        out_shape=[
            jax.ShapeDtypeStruct((r, D), F32),
            jax.ShapeDtypeStruct((r, 128), jnp.int32),
            jax.ShapeDtypeStruct((r, 128), F32),
        ],
        compiler_params=_cparams(("arbitrary",)),
    )(mrow, x, g2, mod, rhi, rlo)


def _row_copy(src_hbm, row, dst_ref, slot, sem):
    return pltpu.make_async_copy(src_hbm.at[pl.ds(row, 1)], dst_ref.at[pl.ds(slot, 1)], sem)


def _dispatch_kernel(src, nused, h_hbm, o_ref, sem):
    i = pl.program_id(0)
    base = i * TM_MOE

    @pl.when(i < nused[0])
    def _():
        def start(rr, c):
            _row_copy(h_hbm, src[base + rr], o_ref, rr, sem).start()
            return c

        lax.fori_loop(0, TM_MOE, start, 0)

        def wait(rr, c):
            _row_copy(h_hbm, 0, o_ref, rr, sem).wait()
            return c

        lax.fori_loop(0, TM_MOE, wait, 0)

    @pl.when(i >= nused[0])
    def _():
        o_ref[...] = jnp.zeros_like(o_ref)


def _dispatch(h, src, nused, n_tiles):
    grid_spec = pltpu.PrefetchScalarGridSpec(
        num_scalar_prefetch=2,
        grid=(n_tiles,),
        in_specs=[pl.BlockSpec(memory_space=pl.ANY)],
        out_specs=pl.BlockSpec((TM_MOE, D), lambda i, s, n: (i, 0)),
        scratch_shapes=[pltpu.SemaphoreType.DMA(())],
    )
    return pl.pallas_call(
        _dispatch_kernel,
        grid_spec=grid_spec,
        out_shape=jax.ShapeDtypeStruct((n_tiles * TM_MOE, D), F32),
        compiler_params=_cparams(("arbitrary",)),
    )(src, nused, h)


def _ffn_grouped_kernel(te, nused, x_ref, wg_ref, wu_ref, wd_ref, o_ref, xb, acc):
    del te
    i = pl.program_id(0)
    f = pl.program_id(1)
    last = pl.num_programs(1) - 1
    used = i < nused[0]

    @pl.when(jnp.logical_and(used, f == 0))
    def _():
        xb[...] = x_ref[...].astype(BF16)
        acc[...] = jnp.zeros_like(acc)

    @pl.when(used)
    def _():
        h = xb[...]
        hid = _silu(_dot(h, wg_ref[...])) * _dot(h, wu_ref[...])
        acc[...] += _dot(hid.astype(BF16), wd_ref[...])

    @pl.when(jnp.logical_and(used, f == last))
    def _():
        o_ref[...] = acc[...]

    @pl.when(jnp.logical_and(jnp.logical_not(used), f == last))
    def _():
        o_ref[...] = jnp.zeros_like(o_ref)


def _ffn_grouped(xg, wg, wu, wd, te, nused, n_tiles):
    nf = D_EXP // TF_MOE
    fidx = lambda i, f, n: jnp.where(i < n[0], f, nf - 1)
    grid_spec = pltpu.PrefetchScalarGridSpec(
        num_scalar_prefetch=2,
        grid=(n_tiles, nf),
        in_specs=[
            pl.BlockSpec((TM_MOE, D), lambda i, f, e, n: (i, 0)),
            pl.BlockSpec((None, D, TF_MOE), lambda i, f, e, n: (e[i], 0, fidx(i, f, n))),
            pl.BlockSpec((None, D, TF_MOE), lambda i, f, e, n: (e[i], 0, fidx(i, f, n))),
            pl.BlockSpec((None, TF_MOE, D), lambda i, f, e, n: (e[i], fidx(i, f, n), 0)),
        ],
        out_specs=pl.BlockSpec((TM_MOE, D), lambda i, f, e, n: (i, 0)),
        scratch_shapes=[pltpu.VMEM((TM_MOE, D), BF16), pltpu.VMEM((TM_MOE, D), F32)],
    )
    return pl.pallas_call(
        _ffn_grouped_kernel,
        grid_spec=grid_spec,
        out_shape=jax.ShapeDtypeStruct((n_tiles * TM_MOE, D), F32),
        compiler_params=_cparams(("arbitrary", "arbitrary")),
    )(te, nused, xg, wg, wu, wd)


def _combine_kernel(mrow, dest, x_ref, mod_ref, rw_ref, y_hbm, o_ref, buf, sem):
    del mrow
    t = pl.program_id(0)
    base = t * TC_COMB * 2

    def start(rr, c):
        _row_copy(y_hbm, dest[base + rr], buf, rr, sem).start()
        return c

    lax.fori_loop(0, 2 * TC_COMB, start, 0)

    def wait(rr, c):
        _row_copy(y_hbm, 0, buf, rr, sem).wait()
        return c

    lax.fori_loop(0, 2 * TC_COMB, wait, 0)
    rw = rw_ref[...]
    y = rw[:, 0:1] * buf[0:TC_COMB, :] + rw[:, 1:2] * buf[TC_COMB:2 * TC_COMB, :]
    o_ref[...] = x_ref[...] + mod_ref[5:6, :] * y


def _combine(x, mod, rw, yg, dest, mrow):
    r = x.shape[0]
    grid_spec = pltpu.PrefetchScalarGridSpec(
        num_scalar_prefetch=2,
        grid=(r // TC_COMB,),
        in_specs=[
            pl.BlockSpec((TC_COMB, D), lambda t, m, d: (t, 0)),
            pl.BlockSpec((None, 6, D), lambda t, m, d: (m[t], 0, 0)),
            pl.BlockSpec((TC_COMB, 128), lambda t, m, d: (t, 0)),
            pl.BlockSpec(memory_space=pl.ANY),
        ],
        out_specs=pl.BlockSpec((TC_COMB, D), lambda t, m, d: (t, 0)),
        scratch_shapes=[pltpu.VMEM((2 * TC_COMB, D), F32), pltpu.SemaphoreType.DMA(())],
    )
    return pl.pallas_call(
        _combine_kernel,
        grid_spec=grid_spec,
        out_shape=jax.ShapeDtypeStruct((r, D), F32),
        compiler_params=_cparams(("arbitrary",)),
    )(mrow, dest, x, mod, rw, yg)


def _moe_plan(ri, n_tiles):
    r = ri.shape[0]
    ef = ri[:, :2].reshape(-1)
    oh = (ef[:, None] == jnp.arange(N_EXP, dtype=jnp.int32)[None, :]).astype(jnp.int32)
    csum = jnp.cumsum(oh, axis=0)
    pos = jnp.take_along_axis(csum, ef[:, None], axis=1)[:, 0] - 1
    counts = csum[-1]
    tiles = (counts + TM_MOE - 1) // TM_MOE
    tile_end = jnp.cumsum(tiles)
    off = (tile_end - tiles) * TM_MOE
    dest = (off[ef] + pos).astype(jnp.int32)
    te = jnp.minimum(jnp.searchsorted(tile_end, jnp.arange(n_tiles, dtype=jnp.int32), side="right"),
                     N_EXP - 1).astype(jnp.int32)
    nused = tile_end[-1:].astype(jnp.int32)
    src = jnp.zeros((n_tiles * TM_MOE,), jnp.int32).at[dest].set(
        jnp.arange(2 * r, dtype=jnp.int32) // 2)
    dest_t = dest.reshape(r // TC_COMB, TC_COMB, 2).transpose(0, 2, 1).reshape(-1)
    return dest_t, src, te, nused


def _mod_rows(n_ctx, n_lat, tile):
    rows = [0] * (n_ctx * SEQ // tile)
    for b in range(n_lat):
        rows += [1 + b] * (TLAT // tile)
    return np.asarray(rows, np.int32)


def _rope_tables():
    t = np.arange(TLAT)
    lane = np.arange(GWP)
    p = lane % GK
    sub = p % (GK // 2)
    nf = GK // 4
    freq = ROPE_BASE ** (-(sub % nf).astype(np.float32) / nf)
    pos = np.where((p < GK // 2)[None, :], (t // GRID_W)[:, None], (t % GRID_W)[:, None]).astype(np.float32)
    ang = jnp.asarray(pos) * jnp.asarray(freq.astype(np.float32))[None, :]
    cos, sin = jnp.cos(ang), jnp.sin(ang)
    lowhalf = jnp.asarray((sub < nf)[None, :])
    sina = jnp.where(lowhalf, -sin, 0.0)
    sinb = jnp.where(lowhalf, 0.0, sin)
    pad1 = jnp.ones((BLK, GWP), F32)
    pad0 = jnp.zeros((BLK, GWP), F32)
    return (jnp.concatenate([cos, pad1]), jnp.concatenate([sina, pad0]), jnp.concatenate([sinb, pad0]))


def _nat_bias_tables(rpb):
    nl = rpb.shape[0]
    col = np.arange(GRID_W)
    c0 = np.clip(col - NAT_KW // 2, 0, GRID_W - NAT_KW)
    in_win = (col[None, :] >= c0[:, None]) & (col[None, :] < c0[:, None] + NAT_KW)
    dc = np.clip(col[None, :] - col[:, None], -(NAT_KW - 1), NAT_KW - 1) + NAT_KW - 1
    cm = jnp.where(jnp.asarray(in_win)[None, None, None], rpb[:, :, :, dc], NEG)
    cm = jnp.concatenate([cm, jnp.full((nl, NH, 1, GRID_W, GRID_W), NEG, F32)], axis=2)
    rows = TLAT // GRID_W
    qrows, krows = TQ // GRID_W, BAND // GRID_W
    idx = np.full((3, qrows, krows), 2 * NAT_KH - 1, np.int64)
    for ty, (row0, ub) in enumerate(((0, 0), (qrows, qrows - krows // 4), (rows - qrows, rows - krows))):
        for a in range(qrows):
            qr = row0 + a
            bs = min(max(qr - NAT_KH // 2, 0), rows - NAT_KH)
            for b in range(krows):
                kr = ub + b
                if bs <= kr < bs + NAT_KH:
                    idx[ty, a, b] = kr - qr + NAT_KH - 1
    t = jnp.take(cm, jnp.asarray(idx.reshape(-1)), axis=2)
    t = t.reshape(nl, NH, 3, qrows, krows, GRID_W, GRID_W)
    return t.transpose(0, 2, 1, 3, 5, 4, 6).reshape(nl, 3, NH, TQ, BAND)


def _heads_to_rows(a):
    b, h, t, d = a.shape
    return a.transpose(0, 2, 1, 3).reshape(b, t, h * d)


def _rows_to_heads(a, b, t):
    return a.reshape(b, t, NH, HD).transpose(0, 2, 1, 3)


def _state_to_blockdiag(s):
    b = s.shape[0]
    eye = jnp.eye(GH, dtype=s.dtype)
    bd = jnp.einsum("bhkv,hg->bhvgk", s, eye).reshape(b, GVW, GW)
    return jnp.pad(bd, ((0, 0), (0, 0), (0, GWP - GW)))


def _blockdiag_to_state(st):
    b = st.shape[0]
    s5 = st[:, :, :GW].reshape(b, GH, GV, GH, GK)
    return jnp.stack([s5[:, h, :, h, :] for h in range(GH)], axis=1).transpose(0, 1, 3, 2)


def kernel(x_prompt, x_sample, cache_nat_k, cache_nat_v, state_gla, c, c_ctx, norm1_g, norm2_g, w_mod, b_mod, w_in, conv_w, gla_gate_w2, gla_gate_b, gla_norm_g, nat_q_norm_g, nat_k_norm_g, nat_rpb, w_out, ffn_w_gate, ffn_w_up, ffn_w_down, moe_router, moe_w_gate, moe_w_up, moe_w_down):
    n_ctx, n_lat = x_prompt.shape[0], x_sample.shape[0]
    depth = w_in.shape[0]
    ncr = n_ctx * SEQ
    r = ncr + n_lat * TLAT
    assert ncr % TLAT == 0 and ncr % TM_FFN == 0 and n_lat + 1 <= 8

    x = jnp.concatenate([x_prompt.reshape(ncr, D), x_sample.reshape(n_lat * TLAT, D)], axis=0)
    cvecs = jnp.zeros((8, D), F32).at[0].set(c_ctx).at[1:1 + n_lat].set(c)
    mod = _modulation(cvecs, w_mod, b_mod)

    z = lambda n: jnp.zeros((depth, D, n), F32)
    w_in_p = jnp.concatenate([
        w_in[:, :, 768:960], z(GWP - GW), w_in[:, :, 960:1152], z(GWP - GW),
        w_in[:, :, 1920:1952], z(128 - 2 * LR),
        w_in[:, :, 0:768], w_in[:, :, 1152:1920], w_in[:, :, 1952:3104]], axis=-1).astype(BF16)
    w_out_b = w_out.astype(BF16)
    w2 = jnp.pad(gla_gate_w2, ((0, 0), (0, 0), (0, 0), (0, GWP - GW)))
    w2f = jnp.pad(w2[:, 0], ((0, 0), (0, 128 - LR), (0, 0))).astype(BF16)
    w2b = jnp.pad(w2[:, 1], ((0, 0), (LR, 128 - 2 * LR), (0, 0))).astype(BF16)
    gb = jnp.pad(gla_gate_b, ((0, 0), (0, 0), (0, GWP - GW)))[:, :, None, :]
    hm = jnp.asarray(np.kron(np.eye(NH), np.full((HD, HD), 1.0 / HD)), BF16)
    qg = jnp.tile(nat_q_norm_g, (1, NH))[:, None, :]
    kg = jnp.tile(nat_k_norm_g, (1, NH))[:, None, :]
    gng = jnp.tile(gla_norm_g, (1, GH))[:, None, :]
    ii = np.arange(BLK)
    same = (ii[:, None] // CHUNK) == (ii[None, :] // CHUNK)
    trif = jnp.asarray(same & (ii[None, :] <= ii[:, None]), BF16)
    trib = jnp.asarray(same & (ii[None, :] >= ii[:, None]), BF16)
    rope = _rope_tables()
    bias = _nat_bias_tables(nat_rpb)
    ck = _heads_to_rows(cache_nat_k.transpose(1, 0, 2, 3, 4).reshape(depth * n_lat, NH, SEQ, HD))
    cv = _heads_to_rows(cache_nat_v.transpose(1, 0, 2, 3, 4).reshape(depth * n_lat, NH, SEQ, HD))
    ck = ck.reshape(depth, n_lat, SEQ, NW).astype(BF16)
    cv = cv.reshape(depth, n_lat, SEQ, NW).astype(BF16)
    zero_state = jnp.zeros((n_ctx, GVW, GWP), F32)
    ffn_g, ffn_u, ffn_d = ffn_w_gate.astype(BF16), ffn_w_up.astype(BF16), ffn_w_down.astype(BF16)
    moe_g, moe_u, moe_d = moe_w_gate.astype(BF16), moe_w_up.astype(BF16), moe_w_down.astype(BF16)
    router_p = jnp.pad(moe_router, ((0, 0), (0, 0), (0, 128 - N_EXP)))
    rhi = router_p.astype(BF16)
    rlo = (router_p - rhi.astype(F32)).astype(BF16)

    mrow_proj = jnp.asarray(_mod_rows(n_ctx, n_lat, TM_PROJ))
    mrow_ffn = jnp.asarray(_mod_rows(n_ctx, n_lat, TM_FFN))
    mrow_comb = jnp.asarray(_mod_rows(n_ctx, n_lat, TC_COMB))
    gla_plan = [jnp.asarray(a) for a in _gla_plan(n_ctx, n_lat)]
    out_plan = [jnp.asarray(a) for a in _proj_out_plan(n_ctx, n_lat)]
    n_tiles = 2 * r // TM_MOE + N_EXP

    new_k, new_v, new_s = [], [], []
    for i in range(depth):
        pg, pc, pv, pn, knf, vnf = _proj_in(x, norm1_g[i][None], mod[i], w_in_p[i], hm, qg[i], kg[i],
                                            mrow_proj, ncr)
        s0f = jnp.concatenate([zero_state, _state_to_blockdiag(state_gla[:, i, 0])], axis=0)
        s0b = jnp.concatenate([zero_state, _state_to_blockdiag(state_gla[:, i, 1])], axis=0)
        o_f, o_b, sff, sfb = _gla(pg, pv, rope, w2f[i], w2b[i], gb[i, 0], gb[i, 1], trif, trib,
                                  s0f, s0b, gla_plan)
        y_ctx = _ctx_attn(pn, n_ctx)
        y_lat = _nat_attn(pn, ck[i], cv[i], bias[i], ncr, n_lat)
        x = _proj_out(x, mod[i], pc, conv_w[i], o_f, o_b, pv, gng[i], hm, y_ctx, y_lat, w_out_b[i],
                      out_plan)
        j = i // 2
        if i % 2 == 0:
            x = _ffn_dense(x, norm2_g[i][None], mod[i], ffn_g[j], ffn_u[j], ffn_d[j], mrow_ffn)
        else:
            h, ri, rw = _router(x, norm2_g[i][None], mod[i], rhi[j], rlo[j], mrow_proj)
            dest, src, te, nused = _moe_plan(ri, n_tiles)
            xg = _dispatch(h, src, nused, n_tiles)
            yg = _ffn_grouped(xg, moe_g[j], moe_u[j], moe_d[j], te, nused, n_tiles)
            x = _combine(x, mod[i], rw, yg, dest, mrow_comb)
        new_k.append(_rows_to_heads(knf[:ncr], n_ctx, SEQ))
        new_v.append(_rows_to_heads(vnf[:ncr], n_ctx, SEQ))
        new_s.append(jnp.stack([_blockdiag_to_state(sff[:n_ctx]), _blockdiag_to_state(sfb[:n_ctx])], axis=1))

    y_prompt = x[:ncr].reshape(n_ctx, SEQ, D)
    y_sample = x[ncr:].reshape(n_lat, TLAT, D)
    return (y_prompt, y_sample, jnp.stack(new_k, axis=1), jnp.stack(new_v, axis=1), jnp.stack(new_s, axis=1))
```

```python
import functools

import numpy as np
import jax
import jax.numpy as jnp
from jax import lax
from jax.experimental import pallas as pl
from jax.experimental.pallas import tpu as pltpu

F32 = jnp.float32
BF16 = jnp.bfloat16

D = 1024
SEQ = 256
TLAT = 4096
GRID_W = 64
HD = 64
CONV_W = 256
GH, GK, GV = 6, 32, 64
NH = 6
LR = 16
GATE_NORM = 16.0
CHUNK = 64
NAT_KH, NAT_KW = 8, 16
ROPE_BASE = 10000.0
D_FF = 2816
N_EXP = 8
D_EXP = 3584
EPS = 1e-6
NEG = -1e30

GW = GH * GK
GWP = 256
GVW = GH * GV
NW = NH * HD

PG_W = 640
PC_W = 768
PV_W = 768
PN_W = 1152
W_IN_P = PG_W + PC_W + PV_W + PN_W

TM_PROJ = 512
BLK = 256
TQ = 512
BAND = 1024
TM_FFN = 1024
TF_FFN = 256
TM_MOE = 512
TF_MOE = 512
TC_COMB = 256
TD_DISP = 256

VMEM_LIMIT = 56 * 1024 * 1024


def _cparams(sem):
    return pltpu.CompilerParams(dimension_semantics=sem, vmem_limit_bytes=VMEM_LIMIT)


def _dot(a, b):
    return jnp.dot(a, b, preferred_element_type=F32)


def _dot_nt(a, b):
    return lax.dot_general(a, b, (((1,), (1,)), ((), ())), preferred_element_type=F32)


def _dot_tn(a, b):
    return lax.dot_general(a, b, (((0,), (0,)), ((), ())), preferred_element_type=F32)


def _split_bf16(a):
    hi = a.astype(BF16)
    lo = (a - hi.astype(F32)).astype(BF16)
    return hi, lo


def _silu(a):
    return a * jax.nn.sigmoid(a)


def _head_mean(sq, hm):
    hi, lo = _split_bf16(sq)
    return _dot(hi, hm) + _dot(lo, hm)


def _mod_kernel(c_ref, w_ref, b_ref, o_ref):
    s = _silu(c_ref[...])
    o_ref[...] = _dot(s.astype(BF16), w_ref[...].astype(BF16)) + b_ref[...]


def _modulation(cvecs, w_mod, b_mod):
    nl = w_mod.shape[0]
    out = pl.pallas_call(
        _mod_kernel,
        grid=(nl, 6),
        in_specs=[
            pl.BlockSpec((8, D), lambda l, j: (0, 0)),
            pl.BlockSpec((None, D, D), lambda l, j: (l, 0, j)),
            pl.BlockSpec((None, 1, D), lambda l, j: (l, 0, j)),
        ],
        out_specs=pl.BlockSpec((None, 8, D), lambda l, j: (l, 0, j)),
        out_shape=jax.ShapeDtypeStruct((nl, 8, 6 * D), F32),
        compiler_params=_cparams(("arbitrary", "arbitrary")),
    )(cvecs, w_mod, b_mod.reshape(nl, 1, 6 * D))
    return out.reshape(nl, 8, 6, D)


def _norm_mod(x, g, shift, scale):
    ms = jnp.mean(x * x, axis=-1, keepdims=True)
    return x * lax.rsqrt(ms + EPS) * g * (1.0 + scale) + shift


def _proj_in_kernel(mrow_ref, x_ref, g_ref, mod_ref, w_ref, hm_ref, qg_ref, kg_ref,
                    pg_ref, pc_ref, pv_ref, pn_ref, knf_ref, vnf_ref):
    del mrow_ref
    h = _norm_mod(x_ref[...], g_ref[...], mod_ref[0:1, :], mod_ref[1:2, :]).astype(BF16)
    pg_ref[...] = _dot(h, w_ref[:, 0:PG_W]).astype(BF16)
    pc_ref[...] = _dot(h, w_ref[:, PG_W:PG_W + PC_W]).astype(BF16)
    pv_ref[...] = _dot(h, w_ref[:, PG_W + PC_W:PG_W + PC_W + PV_W]).astype(BF16)
    o = PG_W + PC_W + PV_W
    nq = _dot(h, w_ref[:, o:o + NW])
    nk = _dot(h, w_ref[:, o + NW:o + 2 * NW])
    nv = _dot(h, w_ref[:, o + 2 * NW:o + 3 * NW])
    hm = hm_ref[...]
    qn = nq * lax.rsqrt(_head_mean(nq * nq, hm) + EPS) * qg_ref[...]
    kn = nk * lax.rsqrt(_head_mean(nk * nk, hm) + EPS) * kg_ref[...]
    pn_ref[:, 0:NW] = (qn * (HD ** -0.5)).astype(BF16)
    pn_ref[:, NW:2 * NW] = kn.astype(BF16)
    pn_ref[:, 2 * NW:3 * NW] = nv.astype(BF16)
    knf_ref[...] = kn
    vnf_ref[...] = nv


def _proj_in(li, x, g1, mod, w_in_p, hm, qg, kg, mrow, n_ctx_rows):
    r = x.shape[0]
    nt = r // TM_PROJ
    nct = n_ctx_rows // TM_PROJ
    row = lambda t, m: (t, 0)
    const = lambda t, m: (0, 0)
    layer = lambda t, m: (li, 0, 0)
    ctx_only = lambda t, m: (jnp.minimum(t, nct), 0)
    grid_spec = pltpu.PrefetchScalarGridSpec(
        num_scalar_prefetch=1,
        grid=(nt,),
        in_specs=[
            pl.BlockSpec((TM_PROJ, D), row),
            pl.BlockSpec((None, 1, D), layer),
            pl.BlockSpec((None, None, 6, D), lambda t, m: (li, m[t], 0, 0)),
            pl.BlockSpec((None, D, W_IN_P), layer),
            pl.BlockSpec((NW, NW), const),
            pl.BlockSpec((None, 1, NW), layer),
            pl.BlockSpec((None, 1, NW), layer),
        ],
        out_specs=[
            pl.BlockSpec((TM_PROJ, PG_W), row),
            pl.BlockSpec((TM_PROJ, PC_W), row),
            pl.BlockSpec((TM_PROJ, PV_W), row),
            pl.BlockSpec((TM_PROJ, PN_W), row),
            pl.BlockSpec((TM_PROJ, NW), ctx_only),
            pl.BlockSpec((TM_PROJ, NW), ctx_only),
        ],
    )
    return pl.pallas_call(
        _proj_in_kernel,
        grid_spec=grid_spec,
        out_shape=[
            jax.ShapeDtypeStruct((r, PG_W), BF16),
            jax.ShapeDtypeStruct((r, PC_W), BF16),
            jax.ShapeDtypeStruct((r, PV_W), BF16),
            jax.ShapeDtypeStruct((r, PN_W), BF16),
            jax.ShapeDtypeStruct((n_ctx_rows + TM_PROJ, NW), F32),
            jax.ShapeDtypeStruct((n_ctx_rows + TM_PROJ, NW), F32),
        ],
        compiler_params=_cparams(("arbitrary",)),
    )(mrow, x, g1, mod, w_in_p, hm, qg, kg)


def _log_sigmoid(z):
    return jnp.minimum(z, 0.0) - jnp.log1p(jnp.exp(-jnp.abs(z)))


def _gla_direction(pg, v, cos, sina, sinb, w2, gb, tri, st_ref, o_ref, rev):
    q = pg[:, 0:GWP].astype(F32) * (GK ** -0.5)
    k = pg[:, GWP:2 * GWP].astype(F32)
    q = q * cos + pltpu.roll(q, GWP - 8, 1) * sina + pltpu.roll(q, 8, 1) * sinb
    k = k * cos + pltpu.roll(k, GWP - 8, 1) * sina + pltpu.roll(k, 8, 1) * sinb
    z = _dot(pg[:, 2 * GWP:2 * GWP + 128], w2) + gb
    g = _log_sigmoid(z) * (1.0 / GATE_NORM)
    ghi, glo = _split_bf16(g)
    cum = _dot(tri, ghi) + _dot(tri, glo)
    qd = (q * jnp.exp(cum)).astype(BF16)
    kd = (k * jnp.exp(-cum)).astype(BF16)

    lane_k = lax.broadcasted_iota(jnp.int32, (1, GWP), 1)
    head_masks = [(lane_k >> 5) == h for h in range(GH)]
    row_a = lax.broadcasted_iota(jnp.int32, (CHUNK, GVW), 0)
    col_a = lax.broadcasted_iota(jnp.int32, (CHUNK, GVW), 1) & (CHUNK - 1)
    keep_a = (col_a >= row_a) if rev else (col_a <= row_a)
    row_v = lax.broadcasted_iota(jnp.int32, (GVW, GVW), 0) >> 6
    col_v = lax.broadcasted_iota(jnp.int32, (GVW, GVW), 1) >> 6
    keep_v = row_v == col_v
    row_s = lax.broadcasted_iota(jnp.int32, (GVW, GWP), 0) >> 6
    col_s = lax.broadcasted_iota(jnp.int32, (GVW, GWP), 1) >> 5
    keep_s = row_s == col_s

    chunks = range(BLK // CHUNK)
    for c in (reversed(chunks) if rev else chunks):
        lo = c * CHUNK
        sl = slice(lo, lo + CHUNK)
        edge = lo if rev else lo + CHUNK - 1
        cend = cum[edge:edge + 1, :]
        kst = (k[sl] * jnp.exp(cend - cum[sl])).astype(BF16)
        decay = jnp.exp(cend)
        kd_c = kd[sl]
        kblk = jnp.concatenate(
            [jnp.where(head_masks[h], kd_c, jnp.zeros_like(kd_c)) for h in range(GH)], axis=0)
        a = jnp.where(keep_a, _dot_nt(qd[sl], kblk), 0.0).astype(BF16)
        v_c = v[sl]
        vblk = jnp.where(keep_v, jnp.concatenate([v_c] * GH, axis=0), jnp.zeros((), BF16))
        st = st_ref[...]
        o = _dot(a, vblk) + _dot_nt(qd[sl], st.astype(BF16))
        o_ref[sl, :] = o.astype(o_ref.dtype)
        ut = _dot_tn(v_c, kst)
        st_ref[...] = st * decay + jnp.where(keep_s, ut, 0.0)


def _gla_kernel(fblk, bblk, seq, first, tblf, tblb,
                pgf_ref, pvf_ref, pgb_ref, pvb_ref,
                cosf_ref, sinaf_ref, sinbf_ref, cosb_ref, sinab_ref, sinbb_ref,
                w2f_ref, w2b_ref, gbf_ref, gbb_ref, trif_ref, trib_ref, s0f_ref, s0b_ref,
                of_ref, ob_ref, sff_ref, sfb_ref, stf, stb):
    del fblk, bblk, seq, tblf, tblb
    u = pl.program_id(0)

    @pl.when(first[u] == 1)
    def _():
        stf[...] = s0f_ref[...]
        stb[...] = s0b_ref[...]

    _gla_direction(pgf_ref[...], pvf_ref[...], cosf_ref[...], sinaf_ref[...], sinbf_ref[...],
                   w2f_ref[...], gbf_ref[...], trif_ref[...], stf, of_ref, False)
    _gla_direction(pgb_ref[...], pvb_ref[...], cosb_ref[...], sinab_ref[...], sinbb_ref[...],
                   w2b_ref[...], gbb_ref[...], trib_ref[...], stb, ob_ref, True)
    sff_ref[...] = stf[...]
    sfb_ref[...] = stb[...]


def _gla_plan(n_ctx, n_lat):
    nb = TLAT // BLK
    fblk, bblk, seq, first, tblf, tblb = [], [], [], [], [], []
    for s in range(n_ctx):
        fblk.append(s); bblk.append(s); seq.append(s); first.append(1)
        tblf.append(nb); tblb.append(nb)
    for s in range(n_lat):
        for j in range(nb):
            fblk.append(n_ctx + s * nb + j)
            bblk.append(n_ctx + s * nb + nb - 1 - j)
            seq.append(n_ctx + s)
            first.append(1 if j == 0 else 0)
            tblf.append(j); tblb.append(nb - 1 - j)
    return [np.asarray(a, np.int32) for a in (fblk, bblk, seq, first, tblf, tblb)]


def _gla(li, pg, pv, rope, w2f, w2b, gb, trif, trib, s0f, s0b, plan):
    r = pg.shape[0]
    nseq = s0f.shape[0]
    nsteps = plan[0].shape[0]
    cos, sina, sinb = rope
    fb = lambda u, f, b, s, fi, tf, tb: (f[u], 0)
    bb = lambda u, f, b, s, fi, tf, tb: (b[u], 0)
    tfm = lambda u, f, b, s, fi, tf, tb: (tf[u], 0)
    tbm = lambda u, f, b, s, fi, tf, tb: (tb[u], 0)
    const = lambda u, f, b, s, fi, tf, tb: (0, 0)
    layer = lambda u, f, b, s, fi, tf, tb: (li, 0, 0)
    sq = lambda u, f, b, s, fi, tf, tb: (s[u], 0, 0)
    tab = pl.BlockSpec((BLK, GWP), tfm)
    tabb = pl.BlockSpec((BLK, GWP), tbm)
    grid_spec = pltpu.PrefetchScalarGridSpec(
        num_scalar_prefetch=6,
        grid=(nsteps,),
        in_specs=[
            pl.BlockSpec((BLK, PG_W), fb), pl.BlockSpec((BLK, GVW), fb),
            pl.BlockSpec((BLK, PG_W), bb), pl.BlockSpec((BLK, GVW), bb),
            tab, tab, tab, tabb, tabb, tabb,
            pl.BlockSpec((None, 128, GWP), layer), pl.BlockSpec((None, 128, GWP), layer),
            pl.BlockSpec((None, None, 1, GWP), lambda u, *_: (li, 0, 0, 0)),
            pl.BlockSpec((None, None, 1, GWP), lambda u, *_: (li, 1, 0, 0)),
            pl.BlockSpec((BLK, BLK), const), pl.BlockSpec((BLK, BLK), const),
            pl.BlockSpec((None, GVW, GWP), sq), pl.BlockSpec((None, GVW, GWP), sq),
        ],
        out_specs=[
            pl.BlockSpec((BLK, GVW), fb), pl.BlockSpec((BLK, GVW), bb),
            pl.BlockSpec((None, GVW, GWP), sq), pl.BlockSpec((None, GVW, GWP), sq),
        ],
        scratch_shapes=[pltpu.VMEM((GVW, GWP), F32), pltpu.VMEM((GVW, GWP), F32)],
    )
    return pl.pallas_call(
        _gla_kernel,
        grid_spec=grid_spec,
        out_shape=[
            jax.ShapeDtypeStruct((r, GVW), BF16), jax.ShapeDtypeStruct((r, GVW), BF16),
            jax.ShapeDtypeStruct((nseq, GVW, GWP), F32), jax.ShapeDtypeStruct((nseq, GVW, GWP), F32),
        ],
        compiler_params=_cparams(("arbitrary",)),
    )(*plan, pg, pv, pg, pv, cos, sina, sinb, cos, sina, sinb,
      w2f, w2b, gb, gb, trif, trib, s0f, s0b)


def _pair_attention(q, keys, vals, biases):
    lane = lax.broadcasted_iota(jnp.int32, (1, 2 * HD), 1)
    first = lane < HD
    outs = []
    for half in range(2):
        qm = jnp.where(first if half == 0 else jnp.logical_not(first), q, jnp.zeros_like(q))
        ss = []
        for kk, bias in zip(keys, biases):
            s = _dot_nt(qm, kk)
            if bias is not None:
                s = s + bias[half]
            ss.append(s)
        m = ss[0].max(axis=-1, keepdims=True)
        for s in ss[1:]:
            m = jnp.maximum(m, s.max(axis=-1, keepdims=True))
        acc = None
        den = None
        for s, vv in zip(ss, vals):
            e = jnp.exp(s - m)
            d = e.sum(axis=-1, keepdims=True)
            o = _dot(e.astype(BF16), vv)
            acc = o if acc is None else acc + o
            den = d if den is None else den + d
        outs.append(acc / den)
    return jnp.where(first, outs[0], outs[1])


def _ctx_attn_kernel(q_ref, k_ref, v_ref, o_ref):
    for p in range(NH // 2):
        sl = slice(p * 2 * HD, (p + 1) * 2 * HD)
        o = _pair_attention(q_ref[:, sl], [k_ref[:, sl]], [v_ref[:, sl]], [None])
        o_ref[:, sl] = o.astype(o_ref.dtype)


def _ctx_attn(pn, n_ctx):
    return pl.pallas_call(
        _ctx_attn_kernel,
        grid=(n_ctx,),
        in_specs=[
            pl.BlockSpec((SEQ, NW), lambda b: (b, 0)),
            pl.BlockSpec((SEQ, NW), lambda b: (b, 1)),
            pl.BlockSpec((SEQ, NW), lambda b: (b, 2)),
        ],
        out_specs=pl.BlockSpec((SEQ, NW), lambda b: (b, 0)),
        out_shape=jax.ShapeDtypeStruct((n_ctx * SEQ, NW), BF16),
        compiler_params=_cparams(("arbitrary",)),
    )(pn, pn, pn)


QROWS = TQ // GRID_W
KPAIRS = BAND // (2 * GRID_W)
N_BIAS_BLK = 31


def _nat_block_table():
    rows = TLAT // GRID_W
    krows = BAND // GRID_W
    tbl = np.zeros((3, QROWS, KPAIRS), np.int32)
    for ty, (row0, ub) in enumerate(((0, 0), (QROWS, QROWS - krows // 4), (rows - QROWS, rows - krows))):
        for a in range(QROWS):
            qr = row0 + a
            bs = min(max(qr - NAT_KH // 2, 0), rows - NAT_KH)
            for kp in range(KPAIRS):
                kr0 = ub + 2 * kp
                v0 = bs <= kr0 < bs + NAT_KH
                v1 = bs <= kr0 + 1 < bs + NAT_KH
                a0 = kr0 - qr + NAT_KH - 1
                if v0 and v1:
                    tbl[ty, a, kp] = a0
                elif v1:
                    tbl[ty, a, kp] = 14 + a0 + 1
                elif v0:
                    tbl[ty, a, kp] = 22 + a0 - (NAT_KH - 1)
                else:
                    tbl[ty, a, kp] = N_BIAS_BLK - 1
    return tbl


_NAT_TBL = _nat_block_table()


def _nat_tile(win, idx_fn, q_ref, kb, vb, ck, cv, pb_ref, o_ref, s_scr, c_scr, e_scr, ec_scr):
    lane = lax.broadcasted_iota(jnp.int32, (1, 2 * HD), 1)
    first = lane < HD
    q = q_ref[...]
    w = 2 * GRID_W
    outs = []
    for half in range(2):
        qm = jnp.where(first if half == 0 else jnp.logical_not(first), q, jnp.zeros_like(q))
        s_scr[...] = _dot_nt(qm, kb)
        c_scr[...] = _dot_nt(qm, ck)
        dens = []
        for qr in range(QROWS):
            rows = slice(qr * GRID_W, (qr + 1) * GRID_W)
            lo, hi = win[qr]
            blocks = [s_scr[rows, kp * w:(kp + 1) * w] + pb_ref[half, idx_fn(qr, kp)]
                      for kp in range(lo, hi)]
            c0, c1 = c_scr[rows, 0:w], c_scr[rows, w:2 * w]
            mm = jnp.maximum(c0, c1)
            for b in blocks:
                mm = jnp.maximum(mm, b)
            m = mm.max(axis=-1, keepdims=True)
            e0, e1 = jnp.exp(c0 - m), jnp.exp(c1 - m)
            ec_scr[rows, 0:w] = e0.astype(BF16)
            ec_scr[rows, w:2 * w] = e1.astype(BF16)
            acc = e0 + e1
            for kp in range(KPAIRS):
                if lo <= kp < hi:
                    e = jnp.exp(blocks[kp - lo] - m)
                    acc = acc + e
                    e_scr[rows, kp * w:(kp + 1) * w] = e.astype(BF16)
                else:
                    e_scr[rows, kp * w:(kp + 1) * w] = jnp.zeros((GRID_W, w), BF16)
            dens.append(acc.sum(axis=-1, keepdims=True))
        o = _dot(e_scr[...], vb) + _dot(ec_scr[...], cv)
        outs.append(o / jnp.concatenate(dens, axis=0))
    o_ref[...] = jnp.where(first, outs[0], outs[1]).astype(o_ref.dtype)


def _nat_kernel(tbl, q_ref, k_ref, v_ref, ck_ref, cv_ref, pb_ref, o_ref, s_scr, c_scr, e_scr, ec_scr):
    j = pl.program_id(2)
    nj = pl.num_programs(2)
    start = pl.multiple_of(jnp.clip(j * TQ - BAND // 4, 0, TLAT - BAND), 256)
    kb = k_ref[pl.ds(start, BAND), :]
    vb = v_ref[pl.ds(start, BAND), :]
    args = (q_ref, kb, vb, ck_ref[...], cv_ref[...], pb_ref, o_ref, s_scr, c_scr, e_scr, ec_scr)
    edge = jnp.logical_or(j == 0, j == nj - 1)

    @pl.when(edge)
    def _():
        ty = jnp.where(j == 0, 0, 2)
        full = [(0, KPAIRS)] * QROWS
        _nat_tile(full, lambda qr, kp: tbl[(ty * QROWS + qr) * KPAIRS + kp], *args)

    @pl.when(jnp.logical_not(edge))
    def _():
        win = []
        for qr in range(QROWS):
            live = [kp for kp in range(KPAIRS) if _NAT_TBL[1, qr, kp] != N_BIAS_BLK - 1]
            win.append((live[0], live[-1] + 1))
        _nat_tile(win, lambda qr, kp: int(_NAT_TBL[1, qr, kp]), *args)


def _nat_attn(li, pn, ck, cv, pb, n_ctx_rows, n_lat):
    nj = TLAT // TQ
    qb0 = n_ctx_rows // TQ
    sb0 = n_ctx_rows // TLAT
    npair = NH // 2
    grid_spec = pltpu.PrefetchScalarGridSpec(
        num_scalar_prefetch=1,
        grid=(n_lat, npair, nj),
        in_specs=[
            pl.BlockSpec((TQ, 2 * HD), lambda b, p, j, t: (qb0 + b * nj + j, p)),
            pl.BlockSpec((TLAT, 2 * HD), lambda b, p, j, t: (sb0 + b, npair + p)),
            pl.BlockSpec((TLAT, 2 * HD), lambda b, p, j, t: (sb0 + b, 2 * npair + p)),
            pl.BlockSpec((None, None, SEQ, 2 * HD), lambda b, p, j, t: (li, b, 0, p)),
            pl.BlockSpec((None, None, SEQ, 2 * HD), lambda b, p, j, t: (li, b, 0, p)),
            pl.BlockSpec((None, 2, N_BIAS_BLK, GRID_W, 2 * GRID_W), lambda b, p, j, t: (li, p, 0, 0, 0)),
        ],
        out_specs=pl.BlockSpec((TQ, 2 * HD), lambda b, p, j, t: (b * nj + j, p)),
        scratch_shapes=[
            pltpu.VMEM((TQ, BAND), F32), pltpu.VMEM((TQ, SEQ), F32),
            pltpu.VMEM((TQ, BAND), BF16), pltpu.VMEM((TQ, SEQ), BF16),
        ],
    )
    return pl.pallas_call(
        _nat_kernel,
        grid_spec=grid_spec,
        out_shape=jax.ShapeDtypeStruct((n_lat * TLAT, NW), BF16),
        compiler_params=_cparams(("arbitrary", "arbitrary", "arbitrary")),
    )(jnp.asarray(_NAT_TBL.reshape(-1)), pn, pn, pn, ck, cv, pb)


def _proj_out_kernel(mrow, isctx, cidx, lidx, hprev, hnext, pblk, nblk,
                     x_ref, mod_ref, pc_ref, pcp_ref, pcn_ref, cw_ref, of_ref, ob_ref, gr_ref,
                     gng_ref, hm_ref, yc_ref, yl_ref, w_ref, o_ref):
    del mrow, cidx, lidx, pblk, nblk
    t = pl.program_id(0)
    pc = pc_ref[...].astype(F32)
    u = pc[:, CONV_W:2 * CONV_W] * pc[:, 2 * CONV_W:3 * CONV_W]
    pp = pcp_ref[7:8, :].astype(F32)
    pn = pcn_ref[0:1, :].astype(F32)
    u_prev_edge = pp[:, CONV_W:2 * CONV_W] * pp[:, 2 * CONV_W:3 * CONV_W] * hprev[t].astype(F32)
    u_next_edge = pn[:, CONV_W:2 * CONV_W] * pn[:, 2 * CONV_W:3 * CONV_W] * hnext[t].astype(F32)
    rows = lax.broadcasted_iota(jnp.int32, (BLK, CONV_W), 0)
    u_prev = jnp.where(rows == 0, u_prev_edge, pltpu.roll(u, 1, 0))
    u_next = jnp.where(rows == BLK - 1, u_next_edge, pltpu.roll(u, BLK - 1, 0))
    y_conv = pc[:, 0:CONV_W] * (cw_ref[0:1, :] * u_prev + cw_ref[1:2, :] * u + cw_ref[2:3, :] * u_next)

    o = of_ref[...].astype(F32) + ob_ref[...].astype(F32)
    on = o * lax.rsqrt(_head_mean(o * o, hm_ref[...]) + EPS) * gng_ref[...]
    y_gla = on * _silu(gr_ref[...].astype(F32))

    y_nat = jnp.where(isctx[t] == 1, yc_ref[...], yl_ref[...])

    y = (_dot(y_conv.astype(BF16), w_ref[0:CONV_W, :])
         + _dot(y_gla.astype(BF16), w_ref[CONV_W:CONV_W + GVW, :])
         + _dot(y_nat, w_ref[CONV_W + GVW:D, :]))
    o_ref[...] = x_ref[...] + mod_ref[2:3, :] * y


def _proj_out_plan(n_ctx, n_lat):
    nb = TLAT // BLK
    nt = n_ctx + n_lat * nb
    mrow, isctx, cidx, lidx, hprev, hnext, pblk, nblk = ([] for _ in range(8))
    per = BLK // 8
    for t in range(nt):
        ctx = t < n_ctx
        tl = t - n_ctx
        mrow.append(0 if ctx else 1 + tl // nb)
        isctx.append(1 if ctx else 0)
        cidx.append(min(t, n_ctx - 1))
        lidx.append(max(tl, 0))
        hprev.append(0 if ctx or tl % nb == 0 else 1)
        hnext.append(0 if ctx or tl % nb == nb - 1 else 1)
        pblk.append(max(t * per - 1, 0))
        nblk.append(min((t + 1) * per, nt * per - 1))
    return [np.asarray(a, np.int32) for a in (mrow, isctx, cidx, lidx, hprev, hnext, pblk, nblk)]


def _proj_out(li, x, mod, pc, conv_w, o_f, o_b, pv, gng, hm, y_ctx, y_lat, w_out, plan):
    r = x.shape[0]
    nt = plan[0].shape[0]
    row = lambda t, *_: (t, 0)
    const = lambda t, *_: (0, 0)
    layer = lambda t, *_: (li, 0, 0)
    grid_spec = pltpu.PrefetchScalarGridSpec(
        num_scalar_prefetch=8,
        grid=(nt,),
        in_specs=[
            pl.BlockSpec((BLK, D), row),
            pl.BlockSpec((None, None, 6, D), lambda t, m, *_: (li, m[t], 0, 0)),
            pl.BlockSpec((BLK, PC_W), row),
            pl.BlockSpec((8, PC_W), lambda t, m, ic, ci, lidx, hp, hn, pb, nb_: (pb[t], 0)),
            pl.BlockSpec((8, PC_W), lambda t, m, ic, ci, lidx, hp, hn, pb, nb_: (nb_[t], 0)),
            pl.BlockSpec((None, 3, CONV_W), layer),
            pl.BlockSpec((BLK, GVW), row),
            pl.BlockSpec((BLK, GVW), row),
            pl.BlockSpec((BLK, GVW), lambda t, *_: (t, 1)),
            pl.BlockSpec((None, 1, GVW), layer),
            pl.BlockSpec((GVW, GVW), const),
            pl.BlockSpec((BLK, NW), lambda t, m, ic, ci, *_: (ci[t], 0)),
            pl.BlockSpec((BLK, NW), lambda t, m, ic, ci, lidx, *_: (lidx[t], 0)),
            pl.BlockSpec((None, D, D), layer),
        ],
        out_specs=pl.BlockSpec((BLK, D), row),
    )
    return pl.pallas_call(
        _proj_out_kernel,
        grid_spec=grid_spec,
        out_shape=jax.ShapeDtypeStruct((r, D), F32),
        compiler_params=_cparams(("arbitrary",)),
    )(*plan, x, mod, pc, pc, pc, conv_w, o_f, o_b, pv, gng, hm, y_ctx, y_lat, w_out)


def _ffn_dense_kernel(mrow, x_ref, g_ref, mod_ref, wg_ref, wu_ref, wd_ref, o_ref, h_scr, acc):
    del mrow
    f = pl.program_id(1)

    @pl.when(f == 0)
    def _():
        h_scr[...] = _norm_mod(x_ref[...], g_ref[...], mod_ref[3:4, :], mod_ref[4:5, :]).astype(BF16)
        acc[...] = jnp.zeros_like(acc)

    h = h_scr[...]
    hid = _silu(_dot(h, wg_ref[...])) * _dot(h, wu_ref[...])
    acc[...] += _dot(hid.astype(BF16), wd_ref[...])

    @pl.when(f == pl.num_programs(1) - 1)
    def _():
        o_ref[...] = x_ref[...] + mod_ref[5:6, :] * acc[...]


def _ffn_dense(li, j, x, g2, mod, wg, wu, wd, mrow):
    r = x.shape[0]
    nf = D_FF // TF_FFN
    grid_spec = pltpu.PrefetchScalarGridSpec(
        num_scalar_prefetch=1,
        grid=(r // TM_FFN, nf),
        in_specs=[
            pl.BlockSpec((TM_FFN, D), lambda t, f, m: (t, 0)),
            pl.BlockSpec((None, 1, D), lambda t, f, m: (li, 0, 0)),
            pl.BlockSpec((None, None, 6, D), lambda t, f, m: (li, m[t], 0, 0)),
            pl.BlockSpec((None, D, TF_FFN), lambda t, f, m: (j, 0, f)),
            pl.BlockSpec((None, D, TF_FFN), lambda t, f, m: (j, 0, f)),
            pl.BlockSpec((None, TF_FFN, D), lambda t, f, m: (j, f, 0)),
        ],
        out_specs=pl.BlockSpec((TM_FFN, D), lambda t, f, m: (t, 0)),
        scratch_shapes=[pltpu.VMEM((TM_FFN, D), BF16), pltpu.VMEM((TM_FFN, D), F32)],
    )
    return pl.pallas_call(
        _ffn_dense_kernel,
        grid_spec=grid_spec,
        out_shape=jax.ShapeDtypeStruct((r, D), F32),
        compiler_params=_cparams(("arbitrary", "arbitrary")),
    )(mrow, x, g2, mod, wg, wu, wd)


def _router_kernel(mrow, x_ref, g_ref, mod_ref, rhi_ref, rlo_ref, h_ref, ri_ref, rw_ref):
    del mrow
    h = _norm_mod(x_ref[...], g_ref[...], mod_ref[3:4, :], mod_ref[4:5, :])
    h_ref[...] = h
    hhi, hlo = _split_bf16(h)
    logits = _dot(hhi, rhi_ref[...]) + _dot(hlo, rhi_ref[...]) + _dot(hhi, rlo_ref[...])
    lane = lax.broadcasted_iota(jnp.int32, logits.shape, 1).astype(F32)
    lg = jnp.where(lane < N_EXP, logits, -jnp.inf)
    m1 = lg.max(axis=-1, keepdims=True)
    i1 = jnp.where(lg == m1, lane, 128.0).min(axis=-1, keepdims=True)
    lg2 = jnp.where(lane == i1, -jnp.inf, lg)
    m2 = lg2.max(axis=-1, keepdims=True)
    i2 = jnp.where(lg2 == m2, lane, 128.0).min(axis=-1, keepdims=True)
    e = jnp.exp(m2 - m1)
    w1 = 1.0 / (1.0 + e)
    w2 = e / (1.0 + e)
    ri_ref[...] = jnp.where(lane == 0.0, i1, jnp.where(lane == 1.0, i2, 0.0)).astype(jnp.int32)
    rw_ref[...] = jnp.where(lane == 0.0, w1, jnp.where(lane == 1.0, w2, 0.0))


def _router(li, j, x, g2, mod, rhi, rlo, mrow):
    r = x.shape[0]
    row = lambda t, m: (t, 0)
    grid_spec = pltpu.PrefetchScalarGridSpec(
        num_scalar_prefetch=1,
        grid=(r // TM_PROJ,),
        in_specs=[
            pl.BlockSpec((TM_PROJ, D), row),
            pl.BlockSpec((None, 1, D), lambda t, m: (li, 0, 0)),
            pl.BlockSpec((None, None, 6, D), lambda t, m: (li, m[t], 0, 0)),
            pl.BlockSpec((None, D, 128), lambda t, m: (j, 0, 0)),
            pl.BlockSpec((None, D, 128), lambda t, m: (j, 0, 0)),
        ],
        out_specs=[
            pl.BlockSpec((TM_PROJ, D), row),
            pl.BlockSpec((TM_PROJ, 128), row),
            pl.BlockSpec((TM_PROJ, 128), row),
        ],
    )
    return pl.pallas_call(
        _router_kernel,
        grid_spec=grid_spec,
        out_shape=[
            jax.ShapeDtypeStruct((r, D), F32),
            jax.ShapeDtypeStruct((r, 128), jnp.int32),
            jax.ShapeDtypeStruct((r, 128), F32),
        ],
        compiler_params=_cparams(("arbitrary",)),
    )(mrow, x, g2, mod, rhi, rlo)


def _row_copy(src_ref, src_row, dst_ref, dst_row, sem):
    return pltpu.make_async_copy(src_ref.at[pl.ds(src_row, 1)], dst_ref.at[pl.ds(dst_row, 1)], sem)


def _dispatch_kernel(dest, pstart, plen, nused, h_ref, xg_hbm, zbuf, sem, zsem):
    t = pl.program_id(0)
    base = t * 2 * TD_DISP
    n_tiles = xg_hbm.shape[0] // TM_MOE

    def start(rr, c):
        _row_copy(h_ref, rr, xg_hbm, dest[base + 2 * rr], sem).start()
        _row_copy(h_ref, rr, xg_hbm, dest[base + 2 * rr + 1], sem).start()
        return c

    lax.fori_loop(0, TD_DISP, start, 0, unroll=8)

    @pl.when(t == 0)
    def _():
        zbuf[...] = jnp.zeros_like(zbuf)
        for e in range(N_EXP):
            n = plen[e]
            s0 = pstart[e]

            def zstart(rr, c):
                _row_copy(zbuf, 0, xg_hbm, s0 + rr, zsem).start()
                return c

            def zwait(rr, c):
                _row_copy(zbuf, 0, xg_hbm, s0, zsem).wait()
                return c

            lax.fori_loop(0, n, zstart, 0)
            lax.fori_loop(0, n, zwait, 0)

        def tile_copy(i):
            return pltpu.make_async_copy(zbuf, xg_hbm.at[pl.ds(i * TM_MOE, TM_MOE)], zsem)

        def tstart(i, c):
            tile_copy(i).start()
            return c

        def twait(i, c):
            tile_copy(i).wait()
            return c

        lax.fori_loop(nused[0], n_tiles, tstart, 0)
        lax.fori_loop(nused[0], n_tiles, twait, 0)

    def wait(rr, c):
        _row_copy(h_ref, 0, xg_hbm, 0, sem).wait()
        return c

    lax.fori_loop(0, 2 * TD_DISP, wait, 0, unroll=8)


def _dispatch(h, dest, pstart, plen, nused, n_tiles):
    r = h.shape[0]
    grid_spec = pltpu.PrefetchScalarGridSpec(
        num_scalar_prefetch=4,
        grid=(r // TD_DISP,),
        in_specs=[pl.BlockSpec((TD_DISP, D), lambda t, *_: (t, 0))],
        out_specs=pl.BlockSpec(memory_space=pl.ANY),
        scratch_shapes=[pltpu.VMEM((TM_MOE, D), F32), pltpu.SemaphoreType.DMA(()),
                        pltpu.SemaphoreType.DMA(())],
    )
    return pl.pallas_call(
        _dispatch_kernel,
        grid_spec=grid_spec,
        out_shape=jax.ShapeDtypeStruct((n_tiles * TM_MOE, D), F32),
        compiler_params=_cparams(("arbitrary",)),
    )(dest, pstart, plen, nused, h)


def _ffn_grouped_kernel(te, nused, x_ref, wg_ref, wu_ref, wd_ref, o_ref, xb, acc):
    del te
    i = pl.program_id(0)
    f = pl.program_id(1)
    last = pl.num_programs(1) - 1
    used = i < nused[0]

    @pl.when(jnp.logical_and(used, f == 0))
    def _():
        xb[...] = x_ref[...].astype(BF16)
        acc[...] = jnp.zeros_like(acc)

    @pl.when(used)
    def _():
        h = xb[...]
        hid = _silu(_dot(h, wg_ref[...])) * _dot(h, wu_ref[...])
        acc[...] += _dot(hid.astype(BF16), wd_ref[...])

    @pl.when(jnp.logical_and(used, f == last))
    def _():
        o_ref[...] = acc[...]

    @pl.when(jnp.logical_and(jnp.logical_not(used), f == last))
    def _():
        o_ref[...] = jnp.zeros_like(o_ref)


def _ffn_grouped(j, xg, wg, wu, wd, te, nused, n_tiles):
    nf = D_EXP // TF_MOE
    fidx = lambda i, f, n: jnp.where(i < n[0], f, nf - 1)
    grid_spec = pltpu.PrefetchScalarGridSpec(
        num_scalar_prefetch=2,
        grid=(n_tiles, nf),
        in_specs=[
            pl.BlockSpec((TM_MOE, D), lambda i, f, e, n: (jnp.minimum(i, n[0] - 1), 0)),
            pl.BlockSpec((None, None, D, TF_MOE), lambda i, f, e, n: (j, e[i], 0, fidx(i, f, n))),
            pl.BlockSpec((None, None, D, TF_MOE), lambda i, f, e, n: (j, e[i], 0, fidx(i, f, n))),
            pl.BlockSpec((None, None, TF_MOE, D), lambda i, f, e, n: (j, e[i], fidx(i, f, n), 0)),
        ],
        out_specs=pl.BlockSpec((TM_MOE, D), lambda i, f, e, n: (i, 0)),
        scratch_shapes=[pltpu.VMEM((TM_MOE, D), BF16), pltpu.VMEM((TM_MOE, D), F32)],
    )
    return pl.pallas_call(
        _ffn_grouped_kernel,
        grid_spec=grid_spec,
        out_shape=jax.ShapeDtypeStruct((n_tiles * TM_MOE, D), F32),
        compiler_params=_cparams(("arbitrary", "arbitrary")),
    )(te, nused, xg, wg, wu, wd)


def _combine_kernel(mrow, dest, x_ref, mod_ref, rw_ref, y_hbm, o_ref, buf, sem):
    del mrow
    t = pl.program_id(0)
    base = t * TC_COMB * 2

    def start(rr, c):
        _row_copy(y_hbm, dest[base + rr], buf, rr, sem).start()
        return c

    lax.fori_loop(0, 2 * TC_COMB, start, 0, unroll=8)

    def wait(rr, c):
        _row_copy(y_hbm, 0, buf, 0, sem).wait()
        return c

    lax.fori_loop(0, 2 * TC_COMB, wait, 0, unroll=8)
    rw = rw_ref[...]
    y = rw[:, 0:1] * buf[0:TC_COMB, :] + rw[:, 1:2] * buf[TC_COMB:2 * TC_COMB, :]
    o_ref[...] = x_ref[...] + mod_ref[5:6, :] * y


def _combine(li, x, mod, rw, yg, dest, mrow):
    r = x.shape[0]
    grid_spec = pltpu.PrefetchScalarGridSpec(
        num_scalar_prefetch=2,
        grid=(r // TC_COMB,),
        in_specs=[
            pl.BlockSpec((TC_COMB, D), lambda t, m, d: (t, 0)),
            pl.BlockSpec((None, None, 6, D), lambda t, m, d: (li, m[t], 0, 0)),
            pl.BlockSpec((TC_COMB, 128), lambda t, m, d: (t, 0)),
            pl.BlockSpec(memory_space=pl.ANY),
        ],
        out_specs=pl.BlockSpec((TC_COMB, D), lambda t, m, d: (t, 0)),
        scratch_shapes=[pltpu.VMEM((2 * TC_COMB, D), F32), pltpu.SemaphoreType.DMA(())],
    )
    return pl.pallas_call(
        _combine_kernel,
        grid_spec=grid_spec,
        out_shape=jax.ShapeDtypeStruct((r, D), F32),
        compiler_params=_cparams(("arbitrary",)),
    )(mrow, dest, x, mod, rw, yg)


def _moe_plan(ri, n_tiles):
    r = ri.shape[0]
    ef = ri[:, :2].reshape(-1)
    oh = (ef[:, None] == jnp.arange(N_EXP, dtype=jnp.int32)[None, :]).astype(jnp.int32)
    csum = jnp.cumsum(oh, axis=0)
    pos = jnp.sum(csum * oh, axis=1) - 1
    counts = csum[-1]
    tiles = (counts + TM_MOE - 1) // TM_MOE
    tile_end = jnp.cumsum(tiles)
    off = (tile_end - tiles) * TM_MOE
    dest = (jnp.sum(off[None, :] * oh, axis=1) + pos).astype(jnp.int32)
    tile_id = jnp.arange(n_tiles, dtype=jnp.int32)
    te = jnp.minimum(jnp.sum((tile_end[None, :] <= tile_id[:, None]).astype(jnp.int32), axis=1), N_EXP - 1)
    nused = tile_end[-1:].astype(jnp.int32)
    pstart = (off + counts).astype(jnp.int32)
    plen = (tiles * TM_MOE - counts).astype(jnp.int32)
    dest_t = dest.reshape(r // TC_COMB, TC_COMB, 2).transpose(0, 2, 1).reshape(-1)
    return dest, dest_t, te.astype(jnp.int32), nused, pstart, plen


def _mod_rows(n_ctx, n_lat, tile):
    rows = [0] * (n_ctx * SEQ // tile)
    for b in range(n_lat):
        rows += [1 + b] * (TLAT // tile)
    return np.asarray(rows, np.int32)


def _rope_tables():
    t = np.arange(TLAT)
    lane = np.arange(GWP)
    p = lane % GK
    sub = p % (GK // 2)
    nf = GK // 4
    freq = ROPE_BASE ** (-(sub % nf).astype(np.float32) / nf)
    pos = np.where((p < GK // 2)[None, :], (t // GRID_W)[:, None], (t % GRID_W)[:, None]).astype(np.float32)
    ang = jnp.asarray(pos) * jnp.asarray(freq.astype(np.float32))[None, :]
    cos, sin = jnp.cos(ang), jnp.sin(ang)
    lowhalf = jnp.asarray((sub < nf)[None, :])
    sina = jnp.where(lowhalf, -sin, 0.0)
    sinb = jnp.where(lowhalf, 0.0, sin)
    pad1 = jnp.ones((BLK, GWP), F32)
    pad0 = jnp.zeros((BLK, GWP), F32)
    return (jnp.concatenate([cos, pad1]), jnp.concatenate([sina, pad0]), jnp.concatenate([sinb, pad0]))


def _nat_bias_blocks(rpb):
    nl = rpb.shape[0]
    col = np.arange(GRID_W)
    c0 = np.clip(col - NAT_KW // 2, 0, GRID_W - NAT_KW)
    in_win = (col[None, :] >= c0[:, None]) & (col[None, :] < c0[:, None] + NAT_KW)
    dc = np.clip(col[None, :] - col[:, None], -(NAT_KW - 1), NAT_KW - 1) + NAT_KW - 1
    cm = jnp.where(jnp.asarray(in_win)[None, None, None], rpb[:, :, :, dc], NEG)
    na = 2 * NAT_KH - 1
    neg = jnp.full((nl, NH, NAT_KH, GRID_W, GRID_W), NEG, F32)
    full = jnp.concatenate([cm[:, :, 0:na - 1], cm[:, :, 1:na]], axis=-1)
    left = jnp.concatenate([neg, cm[:, :, 0:NAT_KH]], axis=-1)
    right = jnp.concatenate([cm[:, :, NAT_KH - 1:na], neg], axis=-1)
    none = jnp.concatenate([neg[:, :, 0:1], neg[:, :, 0:1]], axis=-1)
    return jnp.concatenate([full, left, right, none], axis=2)


def _heads_to_rows(a):
    b, h, t, d = a.shape
    return a.transpose(0, 2, 1, 3).reshape(b, t, h * d)


def _rows_to_heads(a, b, t):
    return a.reshape(b, t, NH, HD).transpose(0, 2, 1, 3)


def _state_to_blockdiag(s):
    b = s.shape[0]
    eye = jnp.eye(GH, dtype=s.dtype)
    bd = jnp.einsum("bhkv,hg->bhvgk", s, eye).reshape(b, GVW, GW)
    return jnp.pad(bd, ((0, 0), (0, 0), (0, GWP - GW)))


def _blockdiag_to_state(st):
    b = st.shape[0]
    s5 = st[:, :, :GW].reshape(b, GH, GV, GH, GK)
    return jnp.stack([s5[:, h, :, h, :] for h in range(GH)], axis=1).transpose(0, 1, 3, 2)


def kernel(x_prompt, x_sample, cache_nat_k, cache_nat_v, state_gla, c, c_ctx, norm1_g, norm2_g, w_mod, b_mod, w_in, conv_w, gla_gate_w2, gla_gate_b, gla_norm_g, nat_q_norm_g, nat_k_norm_g, nat_rpb, w_out, ffn_w_gate, ffn_w_up, ffn_w_down, moe_router, moe_w_gate, moe_w_up, moe_w_down):
    n_ctx, n_lat = x_prompt.shape[0], x_sample.shape[0]
    depth = w_in.shape[0]
    ncr = n_ctx * SEQ
    r = ncr + n_lat * TLAT
    assert ncr % TLAT == 0 and ncr % TM_FFN == 0 and n_lat + 1 <= 8

    x = jnp.concatenate([x_prompt.reshape(ncr, D), x_sample.reshape(n_lat * TLAT, D)], axis=0)
    cvecs = jnp.zeros((8, D), F32).at[0].set(c_ctx).at[1:1 + n_lat].set(c)
    mod = _modulation(cvecs, w_mod, b_mod)

    z = lambda n: jnp.zeros((depth, D, n), F32)
    w_in_p = jnp.concatenate([
        w_in[:, :, 768:960], z(GWP - GW), w_in[:, :, 960:1152], z(GWP - GW),
        w_in[:, :, 1920:1952], z(128 - 2 * LR),
        w_in[:, :, 0:768], w_in[:, :, 1152:1920], w_in[:, :, 1952:3104]], axis=-1).astype(BF16)
    w_out_b = w_out.astype(BF16)
    w2 = jnp.pad(gla_gate_w2, ((0, 0), (0, 0), (0, 0), (0, GWP - GW)))
    w2f = jnp.pad(w2[:, 0], ((0, 0), (0, 128 - LR), (0, 0))).astype(BF16)
    w2b = jnp.pad(w2[:, 1], ((0, 0), (LR, 128 - 2 * LR), (0, 0))).astype(BF16)
    gb = jnp.pad(gla_gate_b, ((0, 0), (0, 0), (0, GWP - GW)))[:, :, None, :]
    hm = jnp.asarray(np.kron(np.eye(NH), np.full((HD, HD), 1.0 / HD)), BF16)
    g1 = norm1_g[:, None, :]
    g2 = norm2_g[:, None, :]
    qg = jnp.tile(nat_q_norm_g, (1, NH))[:, None, :]
    kg = jnp.tile(nat_k_norm_g, (1, NH))[:, None, :]
    gng = jnp.tile(gla_norm_g, (1, GH))[:, None, :]
    ii = np.arange(BLK)
    same = (ii[:, None] // CHUNK) == (ii[None, :] // CHUNK)
    trif = jnp.asarray(same & (ii[None, :] <= ii[:, None]), BF16)
    trib = jnp.asarray(same & (ii[None, :] >= ii[:, None]), BF16)
    rope = _rope_tables()
    pb = _nat_bias_blocks(nat_rpb)
    ck = _heads_to_rows(cache_nat_k.transpose(1, 0, 2, 3, 4).reshape(depth * n_lat, NH, SEQ, HD))
    cv = _heads_to_rows(cache_nat_v.transpose(1, 0, 2, 3, 4).reshape(depth * n_lat, NH, SEQ, HD))
    ck = ck.reshape(depth, n_lat, SEQ, NW).astype(BF16)
    cv = cv.reshape(depth, n_lat, SEQ, NW).astype(BF16)
    zero_state = jnp.zeros((n_ctx, GVW, GWP), F32)
    ffn_g, ffn_u, ffn_d = ffn_w_gate.astype(BF16), ffn_w_up.astype(BF16), ffn_w_down.astype(BF16)
    moe_g, moe_u, moe_d = moe_w_gate.astype(BF16), moe_w_up.astype(BF16), moe_w_down.astype(BF16)
    router_p = jnp.pad(moe_router, ((0, 0), (0, 0), (0, 128 - N_EXP)))
    rhi = router_p.astype(BF16)
    rlo = (router_p - rhi.astype(F32)).astype(BF16)

    mrow_proj = jnp.asarray(_mod_rows(n_ctx, n_lat, TM_PROJ))
    mrow_ffn = jnp.asarray(_mod_rows(n_ctx, n_lat, TM_FFN))
    mrow_comb = jnp.asarray(_mod_rows(n_ctx, n_lat, TC_COMB))
    gla_plan = [jnp.asarray(a) for a in _gla_plan(n_ctx, n_lat)]
    out_plan = [jnp.asarray(a) for a in _proj_out_plan(n_ctx, n_lat)]
    n_tiles = 2 * r // TM_MOE + N_EXP

    new_k, new_v, new_s = [], [], []
    for i in range(depth):
        pg, pc, pv, pn, knf, vnf = _proj_in(i, x, g1, mod, w_in_p, hm, qg, kg, mrow_proj, ncr)
        s0f = jnp.concatenate([zero_state, _state_to_blockdiag(state_gla[:, i, 0])], axis=0)
        s0b = jnp.concatenate([zero_state, _state_to_blockdiag(state_gla[:, i, 1])], axis=0)
        o_f, o_b, sff, sfb = _gla(i, pg, pv, rope, w2f, w2b, gb, trif, trib, s0f, s0b, gla_plan)
        y_ctx = _ctx_attn(pn, n_ctx)
        y_lat = _nat_attn(i, pn, ck, cv, pb, ncr, n_lat)
        x = _proj_out(i, x, mod, pc, conv_w, o_f, o_b, pv, gng, hm, y_ctx, y_lat, w_out_b, out_plan)
        j = i // 2
        if i % 2 == 0:
            x = _ffn_dense(i, j, x, g2, mod, ffn_g, ffn_u, ffn_d, mrow_ffn)
        else:
            h, ri, rw = _router(i, j, x, g2, mod, rhi, rlo, mrow_proj)
            dest, dest_t, te, nused, pstart, plen = _moe_plan(ri, n_tiles)
            xg = _dispatch(h, dest, pstart, plen, nused, n_tiles)
            yg = _ffn_grouped(j, xg, moe_g, moe_u, moe_d, te, nused, n_tiles)
            x = _combine(i, x, mod, rw, yg, dest_t, mrow_comb)
        new_k.append(_rows_to_heads(knf[:ncr], n_ctx, SEQ))
        new_v.append(_rows_to_heads(vnf[:ncr], n_ctx, SEQ))
        new_s.append(jnp.stack([_blockdiag_to_state(sff[:n_ctx]), _blockdiag_to_state(sfb[:n_ctx])], axis=1))

    y_prompt = x[:ncr].reshape(n_ctx, SEQ, D)
    y_sample = x[ncr:].reshape(n_lat, TLAT, D)
    return (y_prompt, y_sample, jnp.stack(new_k, axis=1), jnp.stack(new_v, axis=1), jnp.stack(new_s, axis=1))
```

```python
import numpy as np
import jax
import jax.numpy as jnp
from jax import lax
from jax.experimental import pallas as pl
from jax.experimental.pallas import tpu as pltpu

F32 = jnp.float32
BF16 = jnp.bfloat16

D = 1024
SEQ = 256
TLAT = 4096
GRID_W = 64
HD = 64
CONV_W = 256
GH, GK, GV = 6, 32, 64
NH = 6
LR = 16
GATE_NORM = 16.0
CHUNK = 64
NAT_KH, NAT_KW = 8, 16
ROPE_BASE = 10000.0
D_FF = 2816
N_EXP = 8
D_EXP = 3584
EPS = 1e-6
NEG = -1e30

GW = GH * GK
GWP = 256
GVW = GH * GV
NW = NH * HD

PG_IN = 640
PG_W = 1024
PC_W = 768
PV_W = 768
PN_W = 1152
W_IN_P = PG_IN + PC_W + PV_W + PN_W

TM_PROJ = 512
BLK = 256
TQ = 512
BAND = 1024
FFN_TILES = ((512, 1408), (1024, 256))
MOE_TILES = ((512, 896), (1024, 512))
TC_COMB = 256
TD_DISP = 256

VMEM_LIMIT = 56 * 1024 * 1024


def _cparams(sem):
    return pltpu.CompilerParams(dimension_semantics=sem, vmem_limit_bytes=VMEM_LIMIT)


def _dot(a, b):
    return jnp.dot(a, b, preferred_element_type=F32)


def _dot_nt(a, b):
    return lax.dot_general(a, b, (((1,), (1,)), ((), ())), preferred_element_type=F32)


def _dot_tn(a, b):
    return lax.dot_general(a, b, (((0,), (0,)), ((), ())), preferred_element_type=F32)


def _split_bf16(a):
    hi = a.astype(BF16)
    lo = (a - hi.astype(F32)).astype(BF16)
    return hi, lo


def _silu(a):
    return a * jax.nn.sigmoid(a)


def _head_mean(sq, hm):
    hi, lo = _split_bf16(sq)
    return _dot(hi, hm) + _dot(lo, hm)


def _mod_kernel(c_ref, w_ref, b_ref, o_ref):
    s = _silu(c_ref[...])
    o_ref[...] = _dot(s.astype(BF16), w_ref[...].astype(BF16)) + b_ref[...]


def _modulation(cvecs, w_mod, b_mod):
    nl = w_mod.shape[0]
    out = pl.pallas_call(
        _mod_kernel,
        grid=(nl, 6),
        in_specs=[
            pl.BlockSpec((8, D), lambda l, j: (0, 0)),
            pl.BlockSpec((None, D, D), lambda l, j: (l, 0, j)),
            pl.BlockSpec((None, 1, D), lambda l, j: (l, 0, j)),
        ],
        out_specs=pl.BlockSpec((None, 8, D), lambda l, j: (l, 0, j)),
        out_shape=jax.ShapeDtypeStruct((nl, 8, 6 * D), F32),
        compiler_params=_cparams(("arbitrary", "arbitrary")),
    )(cvecs, w_mod, b_mod.reshape(nl, 1, 6 * D))
    return out.reshape(nl, 8, 6, D)


def _norm_mod(x, g, shift, scale):
    ms = jnp.mean(x * x, axis=-1, keepdims=True)
    return x * lax.rsqrt(ms + EPS) * g * (1.0 + scale) + shift


def _log_sigmoid(z):
    return jnp.minimum(z, 0.0) - jnp.log1p(jnp.exp(-jnp.abs(z)))


def _rope(a, cos, sina, sinb):
    return a * cos + pltpu.roll(a, GWP - 8, 1) * sina + pltpu.roll(a, 8, 1) * sinb


def _proj_in_kernel(rtile, mrow, tbl, kvblk, x_ref, g_ref, mod_ref, w_ref, hm_ref, qg_ref, kg_ref,
                    cos_ref, sina_ref, sinb_ref, w2f_ref, w2b_ref, gbf_ref, gbb_ref,
                    pg_ref, pc_ref, pv_ref, pn_ref, knf_ref, vnf_ref):
    del rtile, mrow, tbl, kvblk
    h = _norm_mod(x_ref[...], g_ref[...], mod_ref[0:1, :], mod_ref[1:2, :]).astype(BF16)
    gq = _dot(h, w_ref[:, 0:GWP]) * (GK ** -0.5)
    gk = _dot(h, w_ref[:, GWP:2 * GWP])
    lr = _dot(h, w_ref[:, 2 * GWP:PG_IN]).astype(BF16)
    cos, sina, sinb = cos_ref[...], sina_ref[...], sinb_ref[...]
    pg_ref[:, 0:GWP] = _rope(gq, cos, sina, sinb).astype(BF16)
    pg_ref[:, GWP:2 * GWP] = _rope(gk, cos, sina, sinb).astype(BF16)
    gf = _log_sigmoid(_dot(lr, w2f_ref[...]) + gbf_ref[...]) * (1.0 / GATE_NORM)
    gb = _log_sigmoid(_dot(lr, w2b_ref[...]) + gbb_ref[...]) * (1.0 / GATE_NORM)
    pg_ref[:, 2 * GWP:3 * GWP] = gf.astype(BF16)
    pg_ref[:, 3 * GWP:4 * GWP] = gb.astype(BF16)
    pc_ref[...] = _dot(h, w_ref[:, PG_IN:PG_IN + PC_W]).astype(BF16)
    pv_ref[...] = _dot(h, w_ref[:, PG_IN + PC_W:PG_IN + PC_W + PV_W]).astype(BF16)
    o = PG_IN + PC_W + PV_W
    nq = _dot(h, w_ref[:, o:o + NW])
    nk = _dot(h, w_ref[:, o + NW:o + 2 * NW])
    nv = _dot(h, w_ref[:, o + 2 * NW:o + 3 * NW])
    hm = hm_ref[...]
    qn = nq * lax.rsqrt(_head_mean(nq * nq, hm) + EPS) * qg_ref[...]
    kn = nk * lax.rsqrt(_head_mean(nk * nk, hm) + EPS) * kg_ref[...]
    pn_ref[:, 0:NW] = (qn * (HD ** -0.5)).astype(BF16)
    pn_ref[:, NW:2 * NW] = kn.astype(BF16)
    pn_ref[:, 2 * NW:3 * NW] = nv.astype(BF16)
    knf_ref[...] = kn
    vnf_ref[...] = nv


def _proj_in_plan(n_ctx, n_lat):
    nct = n_ctx * SEQ // TM_PROJ
    per = TLAT // TM_PROJ
    rtile, mrow, tbl, kvblk = [], [], [], []
    for b in range(n_lat):
        for j in range(per):
            rtile.append(nct + b * per + j); mrow.append(1 + b); tbl.append(j); kvblk.append(0)
    for t in range(nct):
        rtile.append(t); mrow.append(0); tbl.append(per); kvblk.append(t)
    return [np.asarray(a, np.int32) for a in (rtile, mrow, tbl, kvblk)]


def _proj_in(li, x, g1, mod, w_in_p, hm, qg, kg, rope, w2f, w2b, gb, plan, n_ctx_rows):
    r = x.shape[0]
    nt = r // TM_PROJ
    cos, sina, sinb = rope
    row = lambda t, rt, *_: (rt[t], 0)
    const = lambda t, *_: (0, 0)
    layer = lambda t, *_: (li, 0, 0)
    tab = pl.BlockSpec((TM_PROJ, GWP), lambda t, rt, m, tb, kv: (tb[t], 0))
    kvo = pl.BlockSpec((TM_PROJ, NW), lambda t, rt, m, tb, kv: (kv[t], 0))
    grid_spec = pltpu.PrefetchScalarGridSpec(
        num_scalar_prefetch=4,
        grid=(nt,),
        in_specs=[
            pl.BlockSpec((TM_PROJ, D), row),
            pl.BlockSpec((None, 1, D), layer),
            pl.BlockSpec((None, None, 6, D), lambda t, rt, m, *_: (li, m[t], 0, 0)),
            pl.BlockSpec((None, D, W_IN_P), layer),
            pl.BlockSpec((NW, NW), const),
            pl.BlockSpec((None, 1, NW), layer),
            pl.BlockSpec((None, 1, NW), layer),
            tab, tab, tab,
            pl.BlockSpec((None, 128, GWP), layer), pl.BlockSpec((None, 128, GWP), layer),
            pl.BlockSpec((None, None, 1, GWP), lambda t, *_: (li, 0, 0, 0)),
            pl.BlockSpec((None, None, 1, GWP), lambda t, *_: (li, 1, 0, 0)),
        ],
        out_specs=[
            pl.BlockSpec((TM_PROJ, PG_W), row),
            pl.BlockSpec((TM_PROJ, PC_W), row),
            pl.BlockSpec((TM_PROJ, PV_W), row),
            pl.BlockSpec((TM_PROJ, PN_W), row),
            kvo, kvo,
        ],
    )
    return pl.pallas_call(
        _proj_in_kernel,
        grid_spec=grid_spec,
        out_shape=[
            jax.ShapeDtypeStruct((r, PG_W), BF16),
            jax.ShapeDtypeStruct((r, PC_W), BF16),
            jax.ShapeDtypeStruct((r, PV_W), BF16),
            jax.ShapeDtypeStruct((r, PN_W), BF16),
            jax.ShapeDtypeStruct((n_ctx_rows, NW), F32),
            jax.ShapeDtypeStruct((n_ctx_rows, NW), F32),
        ],
        compiler_params=_cparams(("arbitrary",)),
    )(*plan, x, g1, mod, w_in_p, hm, qg, kg, cos, sina, sinb, w2f, w2b, gb, gb)


def _gla_direction(pg, v, tri, km, vm, am, sm, st_ref, o_ref, rev):
    q = pg[:, 0:GWP].astype(F32)
    k = pg[:, GWP:2 * GWP].astype(F32)
    g = pg[:, 3 * GWP:4 * GWP] if rev else pg[:, 2 * GWP:3 * GWP]
    cum = _dot(tri, g)
    qd = (q * jnp.exp(cum)).astype(BF16)
    kd = (k * jnp.exp(-cum)).astype(BF16)
    chunks = range(BLK // CHUNK)
    for c in (reversed(chunks) if rev else chunks):
        lo = c * CHUNK
        sl = slice(lo, lo + CHUNK)
        edge = lo if rev else lo + CHUNK - 1
        cend = cum[edge:edge + 1, :]
        kst = (k[sl] * jnp.exp(cend - cum[sl])).astype(BF16)
        decay = jnp.exp(cend)
        kblk = jnp.concatenate([kd[sl]] * GH, axis=0) * km
        a = jnp.where(am > 0.0, _dot_nt(qd[sl], kblk), 0.0).astype(BF16)
        v_c = v[sl]
        vblk = jnp.concatenate([v_c] * GH, axis=0) * vm
        st = st_ref[...]
        o = _dot(a, vblk) + _dot_nt(qd[sl], st.astype(BF16))
        o_ref[sl, :] = o.astype(o_ref.dtype)
        ut = _dot_tn(v_c, kst)
        st_ref[...] = st * decay + ut * sm


def _gla_kernel(fblk, bblk, seq, first,
                pgf_ref, pvf_ref, pgb_ref, pvb_ref, trif_ref, trib_ref, km_ref, vm_ref,
                amf_ref, amb_ref, sm_ref, s0f_ref, s0b_ref,
                of_ref, ob_ref, sff_ref, sfb_ref, stf, stb):
    del fblk, bblk, seq
    u = pl.program_id(0)

    @pl.when(first[u] == 1)
    def _():
        stf[...] = s0f_ref[...]
        stb[...] = s0b_ref[...]

    km, vm, sm = km_ref[...], vm_ref[...], sm_ref[...]
    _gla_direction(pgf_ref[...], pvf_ref[...], trif_ref[...], km, vm, amf_ref[...], sm, stf, of_ref, False)
    _gla_direction(pgb_ref[...], pvb_ref[...], trib_ref[...], km, vm, amb_ref[...], sm, stb, ob_ref, True)
    sff_ref[...] = stf[...]
    sfb_ref[...] = stb[...]


def _gla_plan(n_ctx, n_lat):
    nb = TLAT // BLK
    fblk, bblk, seq, first = [], [], [], []
    for s in range(n_ctx):
        fblk.append(s); bblk.append(s); seq.append(s); first.append(1)
    for s in range(n_lat):
        for j in range(nb):
            fblk.append(n_ctx + s * nb + j)
            bblk.append(n_ctx + s * nb + nb - 1 - j)
            seq.append(n_ctx + s)
            first.append(1 if j == 0 else 0)
    return [np.asarray(a, np.int32) for a in (fblk, bblk, seq, first)]


def _gla_masks():
    ii = np.arange(BLK)
    same = (ii[:, None] // CHUNK) == (ii[None, :] // CHUNK)
    trif = same & (ii[None, :] <= ii[:, None])
    trib = same & (ii[None, :] >= ii[:, None])
    rk = np.arange(GVW)[:, None] // CHUNK
    km = rk == (np.arange(GWP)[None, :] // GK)
    vm = rk == (np.arange(GVW)[None, :] // GV)
    t = np.arange(CHUNK)[:, None]
    j = np.arange(GVW)[None, :] % CHUNK
    return (jnp.asarray(trif, BF16), jnp.asarray(trib, BF16), jnp.asarray(km, BF16), jnp.asarray(vm, BF16),
            jnp.asarray(j <= t, F32), jnp.asarray(j >= t, F32), jnp.asarray(km, F32))


def _gla(pg, pv, masks, s0f, s0b, plan):
    r = pg.shape[0]
    nseq = s0f.shape[0]
    nsteps = plan[0].shape[0]
    fb = lambda u, f, b, s, fi: (f[u], 0)
    bb = lambda u, f, b, s, fi: (b[u], 0)
    const = lambda u, *_: (0, 0)
    sq = lambda u, f, b, s, fi: (s[u], 0, 0)
    grid_spec = pltpu.PrefetchScalarGridSpec(
        num_scalar_prefetch=4,
        grid=(nsteps,),
        in_specs=[
            pl.BlockSpec((BLK, PG_W), fb), pl.BlockSpec((BLK, GVW), fb),
            pl.BlockSpec((BLK, PG_W), bb), pl.BlockSpec((BLK, GVW), bb),
            pl.BlockSpec((BLK, BLK), const), pl.BlockSpec((BLK, BLK), const),
            pl.BlockSpec((GVW, GWP), const), pl.BlockSpec((GVW, GVW), const),
            pl.BlockSpec((CHUNK, GVW), const), pl.BlockSpec((CHUNK, GVW), const),
            pl.BlockSpec((GVW, GWP), const),
            pl.BlockSpec((None, GVW, GWP), sq), pl.BlockSpec((None, GVW, GWP), sq),
        ],
        out_specs=[
            pl.BlockSpec((BLK, GVW), fb), pl.BlockSpec((BLK, GVW), bb),
            pl.BlockSpec((None, GVW, GWP), sq), pl.BlockSpec((None, GVW, GWP), sq),
        ],
        scratch_shapes=[pltpu.VMEM((GVW, GWP), F32), pltpu.VMEM((GVW, GWP), F32)],
    )
    return pl.pallas_call(
        _gla_kernel,
        grid_spec=grid_spec,
        out_shape=[
            jax.ShapeDtypeStruct((r, GVW), BF16), jax.ShapeDtypeStruct((r, GVW), BF16),
            jax.ShapeDtypeStruct((nseq, GVW, GWP), F32), jax.ShapeDtypeStruct((nseq, GVW, GWP), F32),
        ],
        compiler_params=_cparams(("arbitrary",)),
    )(*plan, pg, pv, pg, pv, *masks, s0f, s0b)


def _pair_attention(q, keys, vals, biases):
    lane = lax.broadcasted_iota(jnp.int32, (1, 2 * HD), 1)
    first = lane < HD
    outs = []
    for half in range(2):
        qm = jnp.where(first if half == 0 else jnp.logical_not(first), q, jnp.zeros_like(q))
        ss = []
        for kk, bias in zip(keys, biases):
            s = _dot_nt(qm, kk)
            if bias is not None:
                s = s + bias[half]
            ss.append(s)
        m = ss[0].max(axis=-1, keepdims=True)
        for s in ss[1:]:
            m = jnp.maximum(m, s.max(axis=-1, keepdims=True))
        acc = None
        den = None
        for s, vv in zip(ss, vals):
            e = jnp.exp(s - m)
            d = e.sum(axis=-1, keepdims=True)
            o = _dot(e.astype(BF16), vv)
            acc = o if acc is None else acc + o
            den = d if den is None else den + d
        outs.append(acc / den)
    return jnp.where(first, outs[0], outs[1])


def _ctx_attn_kernel(q_ref, k_ref, v_ref, o_ref):
    for p in range(NH // 2):
        sl = slice(p * 2 * HD, (p + 1) * 2 * HD)
        o = _pair_attention(q_ref[:, sl], [k_ref[:, sl]], [v_ref[:, sl]], [None])
        o_ref[:, sl] = o.astype(o_ref.dtype)


def _ctx_attn(pn, n_ctx):
    return pl.pallas_call(
        _ctx_attn_kernel,
        grid=(n_ctx,),
        in_specs=[
            pl.BlockSpec((SEQ, NW), lambda b: (b, 0)),
            pl.BlockSpec((SEQ, NW), lambda b: (b, 1)),
            pl.BlockSpec((SEQ, NW), lambda b: (b, 2)),
        ],
        out_specs=pl.BlockSpec((SEQ, NW), lambda b: (b, 0)),
        out_shape=jax.ShapeDtypeStruct((n_ctx * SEQ, NW), BF16),
        compiler_params=_cparams(("arbitrary",)),
    )(pn, pn, pn)


QROWS = TQ // GRID_W
KPAIRS = BAND // (2 * GRID_W)
N_BIAS_BLK = 31


def _nat_block_table():
    rows = TLAT // GRID_W
    krows = BAND // GRID_W
    tbl = np.zeros((3, QROWS, KPAIRS), np.int32)
    for ty, (row0, ub) in enumerate(((0, 0), (QROWS, QROWS - krows // 4), (rows - QROWS, rows - krows))):
        for a in range(QROWS):
            qr = row0 + a
            bs = min(max(qr - NAT_KH // 2, 0), rows - NAT_KH)
            for kp in range(KPAIRS):
                kr0 = ub + 2 * kp
                v0 = bs <= kr0 < bs + NAT_KH
                v1 = bs <= kr0 + 1 < bs + NAT_KH
                a0 = kr0 - qr + NAT_KH - 1
                if v0 and v1:
                    tbl[ty, a, kp] = a0
                elif v1:
                    tbl[ty, a, kp] = 14 + a0 + 1
                elif v0:
                    tbl[ty, a, kp] = 22 + a0 - (NAT_KH - 1)
                else:
                    tbl[ty, a, kp] = N_BIAS_BLK - 1
    return tbl


_NAT_TBL = _nat_block_table()


def _nat_tile(win, idx_fn, q_ref, kb, vb, ck, cv, pb_ref, o_ref, s_scr, c_scr, e_scr, ec_scr):
    lane = lax.broadcasted_iota(jnp.int32, (1, 2 * HD), 1)
    first = lane < HD
    q = q_ref[...]
    w = 2 * GRID_W
    outs = []
    for half in range(2):
        qm = jnp.where(first if half == 0 else jnp.logical_not(first), q, jnp.zeros_like(q))
        s_scr[...] = _dot_nt(qm, kb)
        c_scr[...] = _dot_nt(qm, ck)
        dens = []
        for qr in range(QROWS):
            rows = slice(qr * GRID_W, (qr + 1) * GRID_W)
            lo, hi = win[qr]
            blocks = [s_scr[rows, kp * w:(kp + 1) * w] + pb_ref[half, idx_fn(qr, kp)]
                      for kp in range(lo, hi)]
            c0, c1 = c_scr[rows, 0:w], c_scr[rows, w:2 * w]
            mm = jnp.maximum(c0, c1)
            for b in blocks:
                mm = jnp.maximum(mm, b)
            m = mm.max(axis=-1, keepdims=True)
            e0, e1 = jnp.exp(c0 - m), jnp.exp(c1 - m)
            ec_scr[rows, 0:w] = e0.astype(BF16)
            ec_scr[rows, w:2 * w] = e1.astype(BF16)
            acc = e0 + e1
            for kp in range(KPAIRS):
                if lo <= kp < hi:
                    e = jnp.exp(blocks[kp - lo] - m)
                    acc = acc + e
                    e_scr[rows, kp * w:(kp + 1) * w] = e.astype(BF16)
                else:
                    e_scr[rows, kp * w:(kp + 1) * w] = jnp.zeros((GRID_W, w), BF16)
            dens.append(acc.sum(axis=-1, keepdims=True))
        o = _dot(e_scr[...], vb) + _dot(ec_scr[...], cv)
        outs.append(o / jnp.concatenate(dens, axis=0))
    o_ref[...] = jnp.where(first, outs[0], outs[1]).astype(o_ref.dtype)


def _nat_kernel(tbl, q_ref, k_ref, v_ref, ck_ref, cv_ref, pb_ref, o_ref, s_scr, c_scr, e_scr, ec_scr):
    j = pl.program_id(2)
    nj = pl.num_programs(2)
    start = pl.multiple_of(jnp.clip(j * TQ - BAND // 4, 0, TLAT - BAND), 256)
    kb = k_ref[pl.ds(start, BAND), :]
    vb = v_ref[pl.ds(start, BAND), :]
    args = (q_ref, kb, vb, ck_ref[...], cv_ref[...], pb_ref, o_ref, s_scr, c_scr, e_scr, ec_scr)
    edge = jnp.logical_or(j == 0, j == nj - 1)

    @pl.when(edge)
    def _():
        ty = jnp.where(j == 0, 0, 2)
        full = [(0, KPAIRS)] * QROWS
        _nat_tile(full, lambda qr, kp: tbl[(ty * QROWS + qr) * KPAIRS + kp], *args)

    @pl.when(jnp.logical_not(edge))
    def _():
        win = []
        for qr in range(QROWS):
            live = [kp for kp in range(KPAIRS) if _NAT_TBL[1, qr, kp] != N_BIAS_BLK - 1]
            win.append((live[0], live[-1] + 1))
        _nat_tile(win, lambda qr, kp: int(_NAT_TBL[1, qr, kp]), *args)


def _nat_attn(li, pn, ck, cv, pb, n_ctx_rows, n_lat):
    nj = TLAT // TQ
    qb0 = n_ctx_rows // TQ
    sb0 = n_ctx_rows // TLAT
    npair = NH // 2
    grid_spec = pltpu.PrefetchScalarGridSpec(
        num_scalar_prefetch=1,
        grid=(n_lat, npair, nj),
        in_specs=[
            pl.BlockSpec((TQ, 2 * HD), lambda b, p, j, t: (qb0 + b * nj + j, p)),
            pl.BlockSpec((TLAT, 2 * HD), lambda b, p, j, t: (sb0 + b, npair + p)),
            pl.BlockSpec((TLAT, 2 * HD), lambda b, p, j, t: (sb0 + b, 2 * npair + p)),
            pl.BlockSpec((None, None, SEQ, 2 * HD), lambda b, p, j, t: (li, b, 0, p)),
            pl.BlockSpec((None, None, SEQ, 2 * HD), lambda b, p, j, t: (li, b, 0, p)),
            pl.BlockSpec((None, 2, N_BIAS_BLK, GRID_W, 2 * GRID_W), lambda b, p, j, t: (li, p, 0, 0, 0)),
        ],
        out_specs=pl.BlockSpec((TQ, 2 * HD), lambda b, p, j, t: (b * nj + j, p)),
        scratch_shapes=[
            pltpu.VMEM((TQ, BAND), F32), pltpu.VMEM((TQ, SEQ), F32),
            pltpu.VMEM((TQ, BAND), BF16), pltpu.VMEM((TQ, SEQ), BF16),
        ],
    )
    return pl.pallas_call(
        _nat_kernel,
        grid_spec=grid_spec,
        out_shape=jax.ShapeDtypeStruct((n_lat * TLAT, NW), BF16),
        compiler_params=_cparams(("arbitrary", "arbitrary", "arbitrary")),
    )(jnp.asarray(_NAT_TBL.reshape(-1)), pn, pn, pn, ck, cv, pb)


def _proj_out_kernel(mrow, isctx, cidx, lidx, hprev, hnext, pblk, nblk,
                     x_ref, mod_ref, pc_ref, pcp_ref, pcn_ref, cw_ref, of_ref, ob_ref, gr_ref,
                     gng_ref, hm_ref, yc_ref, yl_ref, w_ref, o_ref):
    del mrow, cidx, lidx, pblk, nblk
    t = pl.program_id(0)
    pc = pc_ref[...].astype(F32)
    u = pc[:, CONV_W:2 * CONV_W] * pc[:, 2 * CONV_W:3 * CONV_W]
    pp = pcp_ref[7:8, :].astype(F32)
    pn = pcn_ref[0:1, :].astype(F32)
    u_prev_edge = pp[:, CONV_W:2 * CONV_W] * pp[:, 2 * CONV_W:3 * CONV_W] * hprev[t].astype(F32)
    u_next_edge = pn[:, CONV_W:2 * CONV_W] * pn[:, 2 * CONV_W:3 * CONV_W] * hnext[t].astype(F32)
    rows = lax.broadcasted_iota(jnp.int32, (BLK, CONV_W), 0)
    u_prev = jnp.where(rows == 0, u_prev_edge, pltpu.roll(u, 1, 0))
    u_next = jnp.where(rows == BLK - 1, u_next_edge, pltpu.roll(u, BLK - 1, 0))
    y_conv = pc[:, 0:CONV_W] * (cw_ref[0:1, :] * u_prev + cw_ref[1:2, :] * u + cw_ref[2:3, :] * u_next)

    o = of_ref[...].astype(F32) + ob_ref[...].astype(F32)
    on = o * lax.rsqrt(_head_mean(o * o, hm_ref[...]) + EPS) * gng_ref[...]
    y_gla = on * _silu(gr_ref[...].astype(F32))

    y_nat = jnp.where(isctx[t] == 1, yc_ref[...], yl_ref[...])

    y = (_dot(y_conv.astype(BF16), w_ref[0:CONV_W, :])
         + _dot(y_gla.astype(BF16), w_ref[CONV_W:CONV_W + GVW, :])
         + _dot(y_nat, w_ref[CONV_W + GVW:D, :]))
    o_ref[...] = x_ref[...] + mod_ref[2:3, :] * y


def _proj_out_plan(n_ctx, n_lat):
    nb = TLAT // BLK
    nt = n_ctx + n_lat * nb
    mrow, isctx, cidx, lidx, hprev, hnext, pblk, nblk = ([] for _ in range(8))
    per = BLK // 8
    for t in range(nt):
        ctx = t < n_ctx
        tl = t - n_ctx
        mrow.append(0 if ctx else 1 + tl // nb)
        isctx.append(1 if ctx else 0)
        cidx.append(min(t, n_ctx - 1))
        lidx.append(max(tl, 0))
        hprev.append(0 if ctx or tl % nb == 0 else 1)
        hnext.append(0 if ctx or tl % nb == nb - 1 else 1)
        pblk.append(max(t * per - 1, 0))
        nblk.append(min((t + 1) * per, nt * per - 1))
    return [np.asarray(a, np.int32) for a in (mrow, isctx, cidx, lidx, hprev, hnext, pblk, nblk)]


def _proj_out(li, x, mod, pc, conv_w, o_f, o_b, pv, gng, hm, y_ctx, y_lat, w_out, plan):
    r = x.shape[0]
    nt = plan[0].shape[0]
    row = lambda t, *_: (t, 0)
    const = lambda t, *_: (0, 0)
    layer = lambda t, *_: (li, 0, 0)
    grid_spec = pltpu.PrefetchScalarGridSpec(
        num_scalar_prefetch=8,
        grid=(nt,),
        in_specs=[
            pl.BlockSpec((BLK, D), row),
            pl.BlockSpec((None, None, 6, D), lambda t, m, *_: (li, m[t], 0, 0)),
            pl.BlockSpec((BLK, PC_W), row),
            pl.BlockSpec((8, PC_W), lambda t, m, ic, ci, lidx, hp, hn, pb, nb_: (pb[t], 0)),
            pl.BlockSpec((8, PC_W), lambda t, m, ic, ci, lidx, hp, hn, pb, nb_: (nb_[t], 0)),
            pl.BlockSpec((None, 3, CONV_W), layer),
            pl.BlockSpec((BLK, GVW), row),
            pl.BlockSpec((BLK, GVW), row),
            pl.BlockSpec((BLK, GVW), lambda t, *_: (t, 1)),
            pl.BlockSpec((None, 1, GVW), layer),
            pl.BlockSpec((GVW, GVW), const),
            pl.BlockSpec((BLK, NW), lambda t, m, ic, ci, *_: (ci[t], 0)),
            pl.BlockSpec((BLK, NW), lambda t, m, ic, ci, lidx, *_: (lidx[t], 0)),
            pl.BlockSpec((None, D, D), layer),
        ],
        out_specs=pl.BlockSpec((BLK, D), row),
    )
    return pl.pallas_call(
        _proj_out_kernel,
        grid_spec=grid_spec,
        out_shape=jax.ShapeDtypeStruct((r, D), F32),
        compiler_params=_cparams(("arbitrary",)),
    )(*plan, x, mod, pc, pc, pc, conv_w, o_f, o_b, pv, gng, hm, y_ctx, y_lat, w_out)


def _ffn_dense_kernel(mrow, x_ref, g_ref, mod_ref, wg_ref, wu_ref, wd_ref, o_ref, h_scr, acc):
    del mrow
    f = pl.program_id(1)

    @pl.when(f == 0)
    def _():
        h_scr[...] = _norm_mod(x_ref[...], g_ref[...], mod_ref[3:4, :], mod_ref[4:5, :]).astype(BF16)
        acc[...] = jnp.zeros_like(acc)

    h = h_scr[...]
    hid = _silu(_dot(h, wg_ref[...])) * _dot(h, wu_ref[...])
    acc[...] += _dot(hid.astype(BF16), wd_ref[...])

    @pl.when(f == pl.num_programs(1) - 1)
    def _():
        o_ref[...] = x_ref[...] + mod_ref[5:6, :] * acc[...]


def _ffn_dense(li, j, tm, tf, x, g2, mod, wg, wu, wd, mrow):
    r = x.shape[0]
    nf = D_FF // tf
    grid_spec = pltpu.PrefetchScalarGridSpec(
        num_scalar_prefetch=1,
        grid=(r // tm, nf),
        in_specs=[
            pl.BlockSpec((tm, D), lambda t, f, m: (t, 0)),
            pl.BlockSpec((None, 1, D), lambda t, f, m: (li, 0, 0)),
            pl.BlockSpec((None, None, 6, D), lambda t, f, m: (li, m[t], 0, 0)),
            pl.BlockSpec((None, D, tf), lambda t, f, m: (j, 0, f)),
            pl.BlockSpec((None, D, tf), lambda t, f, m: (j, 0, f)),
            pl.BlockSpec((None, tf, D), lambda t, f, m: (j, f, 0)),
        ],
        out_specs=pl.BlockSpec((tm, D), lambda t, f, m: (t, 0)),
        scratch_shapes=[pltpu.VMEM((tm, D), BF16), pltpu.VMEM((tm, D), F32)],
    )
    return pl.pallas_call(
        _ffn_dense_kernel,
        grid_spec=grid_spec,
        out_shape=jax.ShapeDtypeStruct((r, D), F32),
        compiler_params=_cparams(("arbitrary", "arbitrary")),
    )(mrow, x, g2, mod, wg, wu, wd)


def _router_kernel(mrow, x_ref, g_ref, mod_ref, rhi_ref, rlo_ref, h_ref, ri_ref, rw_ref):
    del mrow
    h = _norm_mod(x_ref[...], g_ref[...], mod_ref[3:4, :], mod_ref[4:5, :])
    h_ref[...] = h
    hhi, hlo = _split_bf16(h)
    logits = _dot(hhi, rhi_ref[...]) + _dot(hlo, rhi_ref[...]) + _dot(hhi, rlo_ref[...])
    lane = lax.broadcasted_iota(jnp.int32, logits.shape, 1).astype(F32)
    lg = jnp.where(lane < N_EXP, logits, -jnp.inf)
    m1 = lg.max(axis=-1, keepdims=True)
    i1 = jnp.where(lg == m1, lane, 128.0).min(axis=-1, keepdims=True)
    lg2 = jnp.where(lane == i1, -jnp.inf, lg)
    m2 = lg2.max(axis=-1, keepdims=True)
    i2 = jnp.where(lg2 == m2, lane, 128.0).min(axis=-1, keepdims=True)
    e = jnp.exp(m2 - m1)
    w1 = 1.0 / (1.0 + e)
    w2 = e / (1.0 + e)
    ri_ref[...] = jnp.where(lane == 0.0, i1, jnp.where(lane == 1.0, i2, 0.0)).astype(jnp.int32)
    rw_ref[...] = jnp.where(lane == 0.0, w1, jnp.where(lane == 1.0, w2, 0.0))


def _router(li, j, x, g2, mod, rhi, rlo, mrow):
    r = x.shape[0]
    row = lambda t, m: (t, 0)
    grid_spec = pltpu.PrefetchScalarGridSpec(
        num_scalar_prefetch=1,
        grid=(r // TM_PROJ,),
        in_specs=[
            pl.BlockSpec((TM_PROJ, D), row),
            pl.BlockSpec((None, 1, D), lambda t, m: (li, 0, 0)),
            pl.BlockSpec((None, None, 6, D), lambda t, m: (li, m[t], 0, 0)),
            pl.BlockSpec((None, D, 128), lambda t, m: (j, 0, 0)),
            pl.BlockSpec((None, D, 128), lambda t, m: (j, 0, 0)),
        ],
        out_specs=[
            pl.BlockSpec((TM_PROJ, D), row),
            pl.BlockSpec((TM_PROJ, 128), row),
            pl.BlockSpec((TM_PROJ, 128), row),
        ],
    )
    return pl.pallas_call(
        _router_kernel,
        grid_spec=grid_spec,
        out_shape=[
            jax.ShapeDtypeStruct((r, D), F32),
            jax.ShapeDtypeStruct((r, 128), jnp.int32),
            jax.ShapeDtypeStruct((r, 128), F32),
        ],
        compiler_params=_cparams(("arbitrary",)),
    )(mrow, x, g2, mod, rhi, rlo)


def _row_copy(src_ref, src_row, dst_ref, dst_row, sem):
    return pltpu.make_async_copy(src_ref.at[pl.ds(src_row, 1)], dst_ref.at[pl.ds(dst_row, 1)], sem)


def _dispatch_kernel(dest, pstart, plen, nused, h_ref, xg_hbm, zbuf, sem, zsem):
    t = pl.program_id(0)
    base = t * 2 * TD_DISP
    tm = zbuf.shape[0]
    n_tiles = xg_hbm.shape[0] // tm

    def start(rr, c):
        _row_copy(h_ref, rr, xg_hbm, dest[base + 2 * rr], sem).start()
        _row_copy(h_ref, rr, xg_hbm, dest[base + 2 * rr + 1], sem).start()
        return c

    lax.fori_loop(0, TD_DISP, start, 0, unroll=8)

    @pl.when(t == 0)
    def _():
        zbuf[...] = jnp.zeros_like(zbuf)
        for e in range(N_EXP):
            n = plen[e]
            s0 = pstart[e]

            def zstart(rr, c):
                _row_copy(zbuf, 0, xg_hbm, s0 + rr, zsem).start()
                return c

            def zwait(rr, c):
                _row_copy(zbuf, 0, xg_hbm, s0, zsem).wait()
                return c

            lax.fori_loop(0, n, zstart, 0)
            lax.fori_loop(0, n, zwait, 0)

        def tile_copy(i):
            return pltpu.make_async_copy(zbuf, xg_hbm.at[pl.ds(i * tm, tm)], zsem)

        def tstart(i, c):
            tile_copy(i).start()
            return c

        def twait(i, c):
            tile_copy(i).wait()
            return c

        lax.fori_loop(nused[0], n_tiles, tstart, 0)
        lax.fori_loop(nused[0], n_tiles, twait, 0)

    def wait(rr, c):
        _row_copy(h_ref, 0, xg_hbm, 0, sem).wait()
        return c

    lax.fori_loop(0, 2 * TD_DISP, wait, 0, unroll=8)


def _dispatch(tm, h, dest, pstart, plen, nused, n_tiles):
    r = h.shape[0]
    grid_spec = pltpu.PrefetchScalarGridSpec(
        num_scalar_prefetch=4,
        grid=(r // TD_DISP,),
        in_specs=[pl.BlockSpec((TD_DISP, D), lambda t, *_: (t, 0))],
        out_specs=pl.BlockSpec(memory_space=pl.ANY),
        scratch_shapes=[pltpu.VMEM((tm, D), F32), pltpu.SemaphoreType.DMA(()),
                        pltpu.SemaphoreType.DMA(())],
    )
    return pl.pallas_call(
        _dispatch_kernel,
        grid_spec=grid_spec,
        out_shape=jax.ShapeDtypeStruct((n_tiles * tm, D), F32),
        compiler_params=_cparams(("arbitrary",)),
    )(dest, pstart, plen, nused, h)


def _ffn_grouped_kernel(te, nused, x_ref, wg_ref, wu_ref, wd_ref, o_ref, xb, acc):
    del te
    i = pl.program_id(0)
    f = pl.program_id(1)
    last = pl.num_programs(1) - 1
    used = i < nused[0]

    @pl.when(jnp.logical_and(used, f == 0))
    def _():
        xb[...] = x_ref[...].astype(BF16)
        acc[...] = jnp.zeros_like(acc)

    @pl.when(used)
    def _():
        h = xb[...]
        hid = _silu(_dot(h, wg_ref[...])) * _dot(h, wu_ref[...])
        acc[...] += _dot(hid.astype(BF16), wd_ref[...])

    @pl.when(jnp.logical_and(used, f == last))
    def _():
        o_ref[...] = acc[...]

    @pl.when(jnp.logical_and(jnp.logical_not(used), f == last))
    def _():
        o_ref[...] = jnp.zeros_like(o_ref)


def _ffn_grouped(j, tm, tf, xg, wg, wu, wd, te, nused, n_tiles):
    nf = D_EXP // tf
    fidx = lambda i, f, n: jnp.where(i < n[0], f, nf - 1)
    grid_spec = pltpu.PrefetchScalarGridSpec(
        num_scalar_prefetch=2,
        grid=(n_tiles, nf),
        in_specs=[
            pl.BlockSpec((tm, D), lambda i, f, e, n: (jnp.minimum(i, n[0] - 1), 0)),
            pl.BlockSpec((None, None, D, tf), lambda i, f, e, n: (j, e[i], 0, fidx(i, f, n))),
            pl.BlockSpec((None, None, D, tf), lambda i, f, e, n: (j, e[i], 0, fidx(i, f, n))),
            pl.BlockSpec((None, None, tf, D), lambda i, f, e, n: (j, e[i], fidx(i, f, n), 0)),
        ],
        out_specs=pl.BlockSpec((tm, D), lambda i, f, e, n: (i, 0)),
        scratch_shapes=[pltpu.VMEM((tm, D), BF16), pltpu.VMEM((tm, D), F32)],
    )
    return pl.pallas_call(
        _ffn_grouped_kernel,
        grid_spec=grid_spec,
        out_shape=jax.ShapeDtypeStruct((n_tiles * tm, D), F32),
        compiler_params=_cparams(("arbitrary", "arbitrary")),
    )(te, nused, xg, wg, wu, wd)


def _combine_kernel(mrow, dest, x_ref, mod_ref, rw_ref, y_hbm, o_ref, buf, sem):
    del mrow
    t = pl.program_id(0)
    base = t * TC_COMB * 2

    def start(rr, c):
        _row_copy(y_hbm, dest[base + rr], buf, rr, sem).start()
        return c

    lax.fori_loop(0, 2 * TC_COMB, start, 0, unroll=8)

    def wait(rr, c):
        _row_copy(y_hbm, 0, buf, 0, sem).wait()
        return c

    lax.fori_loop(0, 2 * TC_COMB, wait, 0, unroll=8)
    rw = rw_ref[...]
    y = rw[:, 0:1] * buf[0:TC_COMB, :] + rw[:, 1:2] * buf[TC_COMB:2 * TC_COMB, :]
    o_ref[...] = x_ref[...] + mod_ref[5:6, :] * y


def _combine(li, x, mod, rw, yg, dest, mrow):
    r = x.shape[0]
    grid_spec = pltpu.PrefetchScalarGridSpec(
        num_scalar_prefetch=2,
        grid=(r // TC_COMB,),
        in_specs=[
            pl.BlockSpec((TC_COMB, D), lambda t, m, d: (t, 0)),
            pl.BlockSpec((None, None, 6, D), lambda t, m, d: (li, m[t], 0, 0)),
            pl.BlockSpec((TC_COMB, 128), lambda t, m, d: (t, 0)),
            pl.BlockSpec(memory_space=pl.ANY),
        ],
        out_specs=pl.BlockSpec((TC_COMB, D), lambda t, m, d: (t, 0)),
        scratch_shapes=[pltpu.VMEM((2 * TC_COMB, D), F32), pltpu.SemaphoreType.DMA(())],
    )
    return pl.pallas_call(
        _combine_kernel,
        grid_spec=grid_spec,
        out_shape=jax.ShapeDtypeStruct((r, D), F32),
        compiler_params=_cparams(("arbitrary",)),
    )(mrow, dest, x, mod, rw, yg)


def _moe_plan(ri, tm, n_tiles):
    r = ri.shape[0]
    ef = ri[:, :2].reshape(-1)
    oh = (ef[:, None] == jnp.arange(N_EXP, dtype=jnp.int32)[None, :]).astype(jnp.int32)
    csum = jnp.cumsum(oh, axis=0)
    pos = jnp.sum(csum * oh, axis=1) - 1
    counts = csum[-1]
    tiles = (counts + tm - 1) // tm
    tile_end = jnp.cumsum(tiles)
    off = (tile_end - tiles) * tm
    dest = (jnp.sum(off[None, :] * oh, axis=1) + pos).astype(jnp.int32)
    tile_id = jnp.arange(n_tiles, dtype=jnp.int32)
    te = jnp.minimum(jnp.sum((tile_end[None, :] <= tile_id[:, None]).astype(jnp.int32), axis=1), N_EXP - 1)
    nused = tile_end[-1:].astype(jnp.int32)
    pstart = (off + counts).astype(jnp.int32)
    plen = (tiles * tm - counts).astype(jnp.int32)
    dest_t = dest.reshape(r // TC_COMB, TC_COMB, 2).transpose(0, 2, 1).reshape(-1)
    return dest, dest_t, te.astype(jnp.int32), nused, pstart, plen


def _mod_rows(n_ctx, n_lat, tile):
    rows = [0] * (n_ctx * SEQ // tile)
    for b in range(n_lat):
        rows += [1 + b] * (TLAT // tile)
    return np.asarray(rows, np.int32)


def _rope_tables():
    t = np.arange(TLAT)
    lane = np.arange(GWP)
    p = lane % GK
    sub = p % (GK // 2)
    nf = GK // 4
    freq = ROPE_BASE ** (-(sub % nf).astype(np.float32) / nf)
    pos = np.where((p < GK // 2)[None, :], (t // GRID_W)[:, None], (t % GRID_W)[:, None]).astype(np.float32)
    ang = jnp.asarray(pos) * jnp.asarray(freq.astype(np.float32))[None, :]
    cos, sin = jnp.cos(ang), jnp.sin(ang)
    lowhalf = jnp.asarray((sub < nf)[None, :])
    sina = jnp.where(lowhalf, -sin, 0.0)
    sinb = jnp.where(lowhalf, 0.0, sin)
    pad1 = jnp.ones((TM_PROJ, GWP), F32)
    pad0 = jnp.zeros((TM_PROJ, GWP), F32)
    return (jnp.concatenate([cos, pad1]), jnp.concatenate([sina, pad0]), jnp.concatenate([sinb, pad0]))


def _nat_bias_blocks(rpb):
    nl = rpb.shape[0]
    col = np.arange(GRID_W)
    c0 = np.clip(col - NAT_KW // 2, 0, GRID_W - NAT_KW)
    in_win = (col[None, :] >= c0[:, None]) & (col[None, :] < c0[:, None] + NAT_KW)
    dc = np.clip(col[None, :] - col[:, None], -(NAT_KW - 1), NAT_KW - 1) + NAT_KW - 1
    cm = jnp.where(jnp.asarray(in_win)[None, None, None], rpb[:, :, :, dc], NEG)
    na = 2 * NAT_KH - 1
    neg = jnp.full((nl, NH, NAT_KH, GRID_W, GRID_W), NEG, F32)
    full = jnp.concatenate([cm[:, :, 0:na - 1], cm[:, :, 1:na]], axis=-1)
    left = jnp.concatenate([neg, cm[:, :, 0:NAT_KH]], axis=-1)
    right = jnp.concatenate([cm[:, :, NAT_KH - 1:na], neg], axis=-1)
    none = jnp.concatenate([neg[:, :, 0:1], neg[:, :, 0:1]], axis=-1)
    return jnp.concatenate([full, left, right, none], axis=2)


def _heads_to_rows(a):
    b, h, t, d = a.shape
    return a.transpose(0, 2, 1, 3).reshape(b, t, h * d)


def _rows_to_heads(a, b, t):
    return a.reshape(b, t, NH, HD).transpose(0, 2, 1, 3)


def _state_to_blockdiag(s):
    b = s.shape[0]
    eye = jnp.eye(GH, dtype=s.dtype)
    bd = jnp.einsum("bhkv,hg->bhvgk", s, eye).reshape(b, GVW, GW)
    return jnp.pad(bd, ((0, 0), (0, 0), (0, GWP - GW)))


def _blockdiag_to_state(st):
    b = st.shape[0]
    s5 = st[:, :, :GW].reshape(b, GH, GV, GH, GK)
    return jnp.stack([s5[:, h, :, h, :] for h in range(GH)], axis=1).transpose(0, 1, 3, 2)


def kernel(x_prompt, x_sample, cache_nat_k, cache_nat_v, state_gla, c, c_ctx, norm1_g, norm2_g, w_mod, b_mod, w_in, conv_w, gla_gate_w2, gla_gate_b, gla_norm_g, nat_q_norm_g, nat_k_norm_g, nat_rpb, w_out, ffn_w_gate, ffn_w_up, ffn_w_down, moe_router, moe_w_gate, moe_w_up, moe_w_down):
    n_ctx, n_lat = x_prompt.shape[0], x_sample.shape[0]
    depth = w_in.shape[0]
    ncr = n_ctx * SEQ
    r = ncr + n_lat * TLAT
    assert ncr % TLAT == 0 and n_lat + 1 <= 8

    x = jnp.concatenate([x_prompt.reshape(ncr, D), x_sample.reshape(n_lat * TLAT, D)], axis=0)
    cvecs = jnp.zeros((8, D), F32).at[0].set(c_ctx).at[1:1 + n_lat].set(c)
    mod = _modulation(cvecs, w_mod, b_mod)

    z = lambda n: jnp.zeros((depth, D, n), F32)
    w_in_p = jnp.concatenate([
        w_in[:, :, 768:960], z(GWP - GW), w_in[:, :, 960:1152], z(GWP - GW),
        w_in[:, :, 1920:1952], z(128 - 2 * LR),
        w_in[:, :, 0:768], w_in[:, :, 1152:1920], w_in[:, :, 1952:3104]], axis=-1).astype(BF16)
    w_out_b = w_out.astype(BF16)
    w2 = jnp.pad(gla_gate_w2, ((0, 0), (0, 0), (0, 0), (0, GWP - GW)))
    w2f = jnp.pad(w2[:, 0], ((0, 0), (0, 128 - LR), (0, 0))).astype(BF16)
    w2b = jnp.pad(w2[:, 1], ((0, 0), (LR, 128 - 2 * LR), (0, 0))).astype(BF16)
    gb = jnp.pad(gla_gate_b, ((0, 0), (0, 0), (0, GWP - GW)))[:, :, None, :]
    hm = jnp.asarray(np.kron(np.eye(NH), np.full((HD, HD), 1.0 / HD)), BF16)
    g1 = norm1_g[:, None, :]
    g2 = norm2_g[:, None, :]
    qg = jnp.tile(nat_q_norm_g, (1, NH))[:, None, :]
    kg = jnp.tile(nat_k_norm_g, (1, NH))[:, None, :]
    gng = jnp.tile(gla_norm_g, (1, GH))[:, None, :]
    gla_masks = _gla_masks()
    rope = _rope_tables()
    pb = _nat_bias_blocks(nat_rpb)
    ck = _heads_to_rows(cache_nat_k.transpose(1, 0, 2, 3, 4).reshape(depth * n_lat, NH, SEQ, HD))
    cv = _heads_to_rows(cache_nat_v.transpose(1, 0, 2, 3, 4).reshape(depth * n_lat, NH, SEQ, HD))
    ck = ck.reshape(depth, n_lat, SEQ, NW).astype(BF16)
    cv = cv.reshape(depth, n_lat, SEQ, NW).astype(BF16)
    zero_state = jnp.zeros((n_ctx, GVW, GWP), F32)
    ffn_g, ffn_u, ffn_d = ffn_w_gate.astype(BF16), ffn_w_up.astype(BF16), ffn_w_down.astype(BF16)
    moe_g, moe_u, moe_d = moe_w_gate.astype(BF16), moe_w_up.astype(BF16), moe_w_down.astype(BF16)
    router_p = jnp.pad(moe_router, ((0, 0), (0, 0), (0, 128 - N_EXP)))
    rhi = router_p.astype(BF16)
    rlo = (router_p - rhi.astype(F32)).astype(BF16)

    mrow_proj = jnp.asarray(_mod_rows(n_ctx, n_lat, TM_PROJ))
    in_plan = [jnp.asarray(a) for a in _proj_in_plan(n_ctx, n_lat)]
    mrow_comb = jnp.asarray(_mod_rows(n_ctx, n_lat, TC_COMB))
    gla_plan = [jnp.asarray(a) for a in _gla_plan(n_ctx, n_lat)]
    out_plan = [jnp.asarray(a) for a in _proj_out_plan(n_ctx, n_lat)]

    new_k, new_v, new_s = [], [], []
    for i in range(depth):
        pg, pc, pv, pn, knf, vnf = _proj_in(i, x, g1, mod, w_in_p, hm, qg, kg, rope, w2f, w2b, gb,
                                            in_plan, ncr)
        s0f = jnp.concatenate([zero_state, _state_to_blockdiag(state_gla[:, i, 0])], axis=0)
        s0b = jnp.concatenate([zero_state, _state_to_blockdiag(state_gla[:, i, 1])], axis=0)
        o_f, o_b, sff, sfb = _gla(pg, pv, gla_masks, s0f, s0b, gla_plan)
        y_ctx = _ctx_attn(pn, n_ctx)
        y_lat = _nat_attn(i, pn, ck, cv, pb, ncr, n_lat)
        x = _proj_out(i, x, mod, pc, conv_w, o_f, o_b, pv, gng, hm, y_ctx, y_lat, w_out_b, out_plan)
        j = i // 2
        if i % 2 == 0:
            tm, tf = FFN_TILES[j % len(FFN_TILES)]
            mrow_ffn = jnp.asarray(_mod_rows(n_ctx, n_lat, tm))
            x = _ffn_dense(i, j, tm, tf, x, g2, mod, ffn_g, ffn_u, ffn_d, mrow_ffn)
        else:
            tm, tf = MOE_TILES[j % len(MOE_TILES)]
            n_tiles = 2 * r // tm + N_EXP
            h, ri, rw = _router(i, j, x, g2, mod, rhi, rlo, mrow_proj)
            dest, dest_t, te, nused, pstart, plen = _moe_plan(ri, tm, n_tiles)
            xg = _dispatch(tm, h, dest, pstart, plen, nused, n_tiles)
            yg = _ffn_grouped(j, tm, tf, xg, moe_g, moe_u, moe_d, te, nused, n_tiles)
            x = _combine(i, x, mod, rw, yg, dest_t, mrow_comb)
        new_k.append(_rows_to_heads(knf, n_ctx, SEQ))
        new_v.append(_rows_to_heads(vnf, n_ctx, SEQ))
        new_s.append(jnp.stack([_blockdiag_to_state(sff[:n_ctx]), _blockdiag_to_state(sfb[:n_ctx])], axis=1))

    y_prompt = x[:ncr].reshape(n_ctx, SEQ, D)
    y_sample = x[ncr:].reshape(n_lat, TLAT, D)
    return (y_prompt, y_sample, jnp.stack(new_k, axis=1), jnp.stack(new_v, axis=1), jnp.stack(new_s, axis=1))
```

```python
import numpy as np
import jax
import jax.numpy as jnp
from jax import lax
from jax.experimental import pallas as pl
from jax.experimental.pallas import tpu as pltpu

F32 = jnp.float32
BF16 = jnp.bfloat16

D = 1024
SEQ = 256
TLAT = 4096
GRID_W = 64
HD = 64
CONV_W = 256
GH, GK, GV = 6, 32, 64
NH = 6
LR = 16
GATE_NORM = 16.0
CHUNK = 64
NAT_KH, NAT_KW = 8, 16
ROPE_BASE = 10000.0
D_FF = 2816
N_EXP = 8
D_EXP = 3584
EPS = 1e-6
NEG = -1e30

GW = GH * GK
GWP = 256
GVW = GH * GV
NW = NH * HD

PG_IN = 640
PG_W = 1024
PC_W = 768
PV_W = 768
PN_W = 1152
W_IN_P = PG_IN + PC_W + PV_W + PN_W

TM_PROJ = 512
BLK = 256
TM_OUT = 512
TQ = 512
BAND = 1024
FFN_TILE = (512, 1408)
MOE_TILES = ((1024, 896, True), (1024, 512, False))
TC_COMB = 256
TD_DISP = 256

VMEM_LIMIT = 56 * 1024 * 1024


def _cparams(sem):
    return pltpu.CompilerParams(dimension_semantics=sem, vmem_limit_bytes=VMEM_LIMIT)


def _dot(a, b):
    return jnp.dot(a, b, preferred_element_type=F32)


def _dot_nt(a, b):
    return lax.dot_general(a, b, (((1,), (1,)), ((), ())), preferred_element_type=F32)


def _dot_tn(a, b):
    return lax.dot_general(a, b, (((0,), (0,)), ((), ())), preferred_element_type=F32)


def _split_bf16(a):
    hi = a.astype(BF16)
    lo = (a - hi.astype(F32)).astype(BF16)
    return hi, lo


def _silu(a):
    return a * jax.nn.sigmoid(a)


def _head_mean(sq, hm):
    hi, lo = _split_bf16(sq)
    return _dot(hi, hm) + _dot(lo, hm)


def _mod_kernel(c_ref, w_ref, b_ref, o_ref):
    s = _silu(c_ref[...])
    o_ref[...] = _dot(s.astype(BF16), w_ref[...].astype(BF16)) + b_ref[...]


def _modulation(cvecs, w_mod, b_mod):
    nl = w_mod.shape[0]
    out = pl.pallas_call(
        _mod_kernel,
        grid=(nl, 6),
        in_specs=[
            pl.BlockSpec((8, D), lambda l, j: (0, 0)),
            pl.BlockSpec((None, D, D), lambda l, j: (l, 0, j)),
            pl.BlockSpec((None, 1, D), lambda l, j: (l, 0, j)),
        ],
        out_specs=pl.BlockSpec((None, 8, D), lambda l, j: (l, 0, j)),
        out_shape=jax.ShapeDtypeStruct((nl, 8, 6 * D), F32),
        compiler_params=_cparams(("arbitrary", "arbitrary")),
    )(cvecs, w_mod, b_mod.reshape(nl, 1, 6 * D))
    return out.reshape(nl, 8, 6, D)


def _norm_mod(x, g, shift, scale):
    ms = jnp.mean(x * x, axis=-1, keepdims=True)
    return x * lax.rsqrt(ms + EPS) * g * (1.0 + scale) + shift


def _log_sigmoid(z):
    return jnp.minimum(z, 0.0) - jnp.log1p(jnp.exp(-jnp.abs(z)))


def _rope(a, cos, sina, sinb):
    return a * cos + pltpu.roll(a, GWP - 8, 1) * sina + pltpu.roll(a, 8, 1) * sinb


def _proj_in_kernel(rtile, mrow, tbl, kvblk, x_ref, g_ref, mod_ref, w_ref, hm_ref, qg_ref, kg_ref,
                    cos_ref, sina_ref, sinb_ref, w2f_ref, w2b_ref, gbf_ref, gbb_ref,
                    pg_ref, pc_ref, pv_ref, pn_ref, knf_ref, vnf_ref):
    del rtile, mrow, tbl, kvblk
    h = _norm_mod(x_ref[...], g_ref[...], mod_ref[0:1, :], mod_ref[1:2, :]).astype(BF16)
    gq = _dot(h, w_ref[:, 0:GWP]) * (GK ** -0.5)
    gk = _dot(h, w_ref[:, GWP:2 * GWP])
    lr = _dot(h, w_ref[:, 2 * GWP:PG_IN]).astype(BF16)
    cos, sina, sinb = cos_ref[...], sina_ref[...], sinb_ref[...]
    pg_ref[:, 0:GWP] = _rope(gq, cos, sina, sinb).astype(BF16)
    pg_ref[:, GWP:2 * GWP] = _rope(gk, cos, sina, sinb).astype(BF16)
    gf = _log_sigmoid(_dot(lr, w2f_ref[...]) + gbf_ref[...]) * (1.0 / GATE_NORM)
    gb = _log_sigmoid(_dot(lr, w2b_ref[...]) + gbb_ref[...]) * (1.0 / GATE_NORM)
    pg_ref[:, 2 * GWP:3 * GWP] = gf.astype(BF16)
    pg_ref[:, 3 * GWP:4 * GWP] = gb.astype(BF16)
    pc_ref[...] = _dot(h, w_ref[:, PG_IN:PG_IN + PC_W]).astype(BF16)
    pv_ref[...] = _dot(h, w_ref[:, PG_IN + PC_W:PG_IN + PC_W + PV_W]).astype(BF16)
    o = PG_IN + PC_W + PV_W
    nq = _dot(h, w_ref[:, o:o + NW])
    nk = _dot(h, w_ref[:, o + NW:o + 2 * NW])
    nv = _dot(h, w_ref[:, o + 2 * NW:o + 3 * NW])
    hm = hm_ref[...]
    qn = nq * lax.rsqrt(_head_mean(nq * nq, hm) + EPS) * qg_ref[...]
    kn = nk * lax.rsqrt(_head_mean(nk * nk, hm) + EPS) * kg_ref[...]
    pn_ref[:, 0:NW] = (qn * (HD ** -0.5)).astype(BF16)
    pn_ref[:, NW:2 * NW] = kn.astype(BF16)
    pn_ref[:, 2 * NW:3 * NW] = nv.astype(BF16)
    knf_ref[...] = kn
    vnf_ref[...] = nv


def _proj_in_plan(n_ctx, n_lat):
    nct = n_ctx * SEQ // TM_PROJ
    per = TLAT // TM_PROJ
    rtile, mrow, tbl, kvblk = [], [], [], []
    for b in range(n_lat):
        for j in range(per):
            rtile.append(nct + b * per + j); mrow.append(1 + b); tbl.append(j); kvblk.append(0)
    for t in range(nct):
        rtile.append(t); mrow.append(0); tbl.append(per); kvblk.append(t)
    return [np.asarray(a, np.int32) for a in (rtile, mrow, tbl, kvblk)]


def _proj_in(li, x, g1, mod, w_in_p, hm, qg, kg, rope, w2f, w2b, gb, plan, n_ctx_rows):
    r = x.shape[0]
    nt = r // TM_PROJ
    cos, sina, sinb = rope
    row = lambda t, rt, *_: (rt[t], 0)
    const = lambda t, *_: (0, 0)
    layer = lambda t, *_: (li, 0, 0)
    tab = pl.BlockSpec((TM_PROJ, GWP), lambda t, rt, m, tb, kv: (tb[t], 0))
    kvo = pl.BlockSpec((TM_PROJ, NW), lambda t, rt, m, tb, kv: (kv[t], 0))
    grid_spec = pltpu.PrefetchScalarGridSpec(
        num_scalar_prefetch=4,
        grid=(nt,),
        in_specs=[
            pl.BlockSpec((TM_PROJ, D), row),
            pl.BlockSpec((None, 1, D), layer),
            pl.BlockSpec((None, None, 6, D), lambda t, rt, m, *_: (li, m[t], 0, 0)),
            pl.BlockSpec((None, D, W_IN_P), layer),
            pl.BlockSpec((NW, NW), const),
            pl.BlockSpec((None, 1, NW), layer),
            pl.BlockSpec((None, 1, NW), layer),
            tab, tab, tab,
            pl.BlockSpec((None, 128, GWP), layer), pl.BlockSpec((None, 128, GWP), layer),
            pl.BlockSpec((None, None, 1, GWP), lambda t, *_: (li, 0, 0, 0)),
            pl.BlockSpec((None, None, 1, GWP), lambda t, *_: (li, 1, 0, 0)),
        ],
        out_specs=[
            pl.BlockSpec((TM_PROJ, PG_W), row),
            pl.BlockSpec((TM_PROJ, PC_W), row),
            pl.BlockSpec((TM_PROJ, PV_W), row),
            pl.BlockSpec((TM_PROJ, PN_W), row),
            kvo, kvo,
        ],
    )
    return pl.pallas_call(
        _proj_in_kernel,
        grid_spec=grid_spec,
        out_shape=[
            jax.ShapeDtypeStruct((r, PG_W), BF16),
            jax.ShapeDtypeStruct((r, PC_W), BF16),
            jax.ShapeDtypeStruct((r, PV_W), BF16),
            jax.ShapeDtypeStruct((r, PN_W), BF16),
            jax.ShapeDtypeStruct((n_ctx_rows, NW), F32),
            jax.ShapeDtypeStruct((n_ctx_rows, NW), F32),
        ],
        compiler_params=_cparams(("arbitrary",)),
    )(*plan, x, g1, mod, w_in_p, hm, qg, kg, cos, sina, sinb, w2f, w2b, gb, gb)


def _gla_direction(pg, v, tri, km, vm, am, sm, st_ref, o_ref, rev):
    q = pg[:, 0:GWP].astype(F32)
    k = pg[:, GWP:2 * GWP].astype(F32)
    g = pg[:, 3 * GWP:4 * GWP] if rev else pg[:, 2 * GWP:3 * GWP]
    cum = _dot(tri, g)
    qd = (q * jnp.exp(cum)).astype(BF16)
    kd = (k * jnp.exp(-cum)).astype(BF16)
    chunks = range(BLK // CHUNK)
    for c in (reversed(chunks) if rev else chunks):
        lo = c * CHUNK
        sl = slice(lo, lo + CHUNK)
        edge = lo if rev else lo + CHUNK - 1
        cend = cum[edge:edge + 1, :]
        kst = (k[sl] * jnp.exp(cend - cum[sl])).astype(BF16)
        decay = jnp.exp(cend)
        kblk = jnp.concatenate([kd[sl]] * GH, axis=0) * km
        a = jnp.where(am > 0.0, _dot_nt(qd[sl], kblk), 0.0).astype(BF16)
        v_c = v[sl]
        vblk = jnp.concatenate([v_c] * GH, axis=0) * vm
        st = st_ref[...]
        o = _dot(a, vblk) + _dot_nt(qd[sl], st.astype(BF16))
        o_ref[sl, :] = o.astype(o_ref.dtype)
        ut = _dot_tn(v_c, kst)
        st_ref[...] = st * decay + ut * sm


def _gla_kernel(fblk, bblk, seq, first,
                pgf_ref, pvf_ref, pgb_ref, pvb_ref, trif_ref, trib_ref, km_ref, vm_ref,
                amf_ref, amb_ref, sm_ref, s0f_ref, s0b_ref,
                of_ref, ob_ref, sff_ref, sfb_ref, stf, stb):
    del fblk, bblk, seq
    u = pl.program_id(0)

    @pl.when(first[u] == 1)
    def _():
        stf[...] = s0f_ref[...]
        stb[...] = s0b_ref[...]

    km, vm, sm = km_ref[...], vm_ref[...], sm_ref[...]
    _gla_direction(pgf_ref[...], pvf_ref[...], trif_ref[...], km, vm, amf_ref[...], sm, stf, of_ref, False)
    _gla_direction(pgb_ref[...], pvb_ref[...], trib_ref[...], km, vm, amb_ref[...], sm, stb, ob_ref, True)
    sff_ref[...] = stf[...]
    sfb_ref[...] = stb[...]


def _gla_plan(n_ctx, n_lat):
    nb = TLAT // BLK
    fblk, bblk, seq, first = [], [], [], []
    for s in range(n_ctx):
        fblk.append(s); bblk.append(s); seq.append(s); first.append(1)
    for s in range(n_lat):
        for j in range(nb):
            fblk.append(n_ctx + s * nb + j)
            bblk.append(n_ctx + s * nb + nb - 1 - j)
            seq.append(n_ctx + s)
            first.append(1 if j == 0 else 0)
    return [np.asarray(a, np.int32) for a in (fblk, bblk, seq, first)]


def _gla_masks():
    ii = np.arange(BLK)
    same = (ii[:, None] // CHUNK) == (ii[None, :] // CHUNK)
    trif = same & (ii[None, :] <= ii[:, None])
    trib = same & (ii[None, :] >= ii[:, None])
    rk = np.arange(GVW)[:, None] // CHUNK
    km = rk == (np.arange(GWP)[None, :] // GK)
    vm = rk == (np.arange(GVW)[None, :] // GV)
    t = np.arange(CHUNK)[:, None]
    j = np.arange(GVW)[None, :] % CHUNK
    return (jnp.asarray(trif, BF16), jnp.asarray(trib, BF16), jnp.asarray(km, BF16), jnp.asarray(vm, BF16),
            jnp.asarray(j <= t, F32), jnp.asarray(j >= t, F32), jnp.asarray(km, F32))


def _gla(pg, pv, masks, s0f, s0b, plan):
    r = pg.shape[0]
    nseq = s0f.shape[0]
    nsteps = plan[0].shape[0]
    fb = lambda u, f, b, s, fi: (f[u], 0)
    bb = lambda u, f, b, s, fi: (b[u], 0)
    const = lambda u, *_: (0, 0)
    sq = lambda u, f, b, s, fi: (s[u], 0, 0)
    grid_spec = pltpu.PrefetchScalarGridSpec(
        num_scalar_prefetch=4,
        grid=(nsteps,),
        in_specs=[
            pl.BlockSpec((BLK, PG_W), fb), pl.BlockSpec((BLK, GVW), fb),
            pl.BlockSpec((BLK, PG_W), bb), pl.BlockSpec((BLK, GVW), bb),
            pl.BlockSpec((BLK, BLK), const), pl.BlockSpec((BLK, BLK), const),
            pl.BlockSpec((GVW, GWP), const), pl.BlockSpec((GVW, GVW), const),
            pl.BlockSpec((CHUNK, GVW), const), pl.BlockSpec((CHUNK, GVW), const),
            pl.BlockSpec((GVW, GWP), const),
            pl.BlockSpec((None, GVW, GWP), sq), pl.BlockSpec((None, GVW, GWP), sq),
        ],
        out_specs=[
            pl.BlockSpec((BLK, GVW), fb), pl.BlockSpec((BLK, GVW), bb),
            pl.BlockSpec((None, GVW, GWP), sq), pl.BlockSpec((None, GVW, GWP), sq),
        ],
        scratch_shapes=[pltpu.VMEM((GVW, GWP), F32), pltpu.VMEM((GVW, GWP), F32)],
    )
    return pl.pallas_call(
        _gla_kernel,
        grid_spec=grid_spec,
        out_shape=[
            jax.ShapeDtypeStruct((r, GVW), BF16), jax.ShapeDtypeStruct((r, GVW), BF16),
            jax.ShapeDtypeStruct((nseq, GVW, GWP), F32), jax.ShapeDtypeStruct((nseq, GVW, GWP), F32),
        ],
        compiler_params=_cparams(("arbitrary",)),
    )(*plan, pg, pv, pg, pv, *masks, s0f, s0b)


def _pair_attention(q, keys, vals, biases):
    lane = lax.broadcasted_iota(jnp.int32, (1, 2 * HD), 1)
    first = lane < HD
    outs = []
    for half in range(2):
        qm = jnp.where(first if half == 0 else jnp.logical_not(first), q, jnp.zeros_like(q))
        ss = []
        for kk, bias in zip(keys, biases):
            s = _dot_nt(qm, kk)
            if bias is not None:
                s = s + bias[half]
            ss.append(s)
        m = ss[0].max(axis=-1, keepdims=True)
        for s in ss[1:]:
            m = jnp.maximum(m, s.max(axis=-1, keepdims=True))
        acc = None
        den = None
        for s, vv in zip(ss, vals):
            e = jnp.exp(s - m)
            d = e.sum(axis=-1, keepdims=True)
            o = _dot(e.astype(BF16), vv)
            acc = o if acc is None else acc + o
            den = d if den is None else den + d
        outs.append(acc / den)
    return jnp.where(first, outs[0], outs[1])


def _ctx_attn_kernel(q_ref, k_ref, v_ref, o_ref):
    for p in range(NH // 2):
        sl = slice(p * 2 * HD, (p + 1) * 2 * HD)
        o = _pair_attention(q_ref[:, sl], [k_ref[:, sl]], [v_ref[:, sl]], [None])
        o_ref[:, sl] = o.astype(o_ref.dtype)


def _ctx_attn(pn, n_ctx):
    return pl.pallas_call(
        _ctx_attn_kernel,
        grid=(n_ctx,),
        in_specs=[
            pl.BlockSpec((SEQ, NW), lambda b: (b, 0)),
            pl.BlockSpec((SEQ, NW), lambda b: (b, 1)),
            pl.BlockSpec((SEQ, NW), lambda b: (b, 2)),
        ],
        out_specs=pl.BlockSpec((SEQ, NW), lambda b: (b, 0)),
        out_shape=jax.ShapeDtypeStruct((n_ctx * SEQ, NW), BF16),
        compiler_params=_cparams(("arbitrary",)),
    )(pn, pn, pn)


QROWS = TQ // GRID_W
KPAIRS = BAND // (2 * GRID_W)
N_BIAS_BLK = 31


def _nat_block_table():
    rows = TLAT // GRID_W
    krows = BAND // GRID_W
    tbl = np.zeros((3, QROWS, KPAIRS), np.int32)
    for ty, (row0, ub) in enumerate(((0, 0), (QROWS, QROWS - krows // 4), (rows - QROWS, rows - krows))):
        for a in range(QROWS):
            qr = row0 + a
            bs = min(max(qr - NAT_KH // 2, 0), rows - NAT_KH)
            for kp in range(KPAIRS):
                kr0 = ub + 2 * kp
                v0 = bs <= kr0 < bs + NAT_KH
                v1 = bs <= kr0 + 1 < bs + NAT_KH
                a0 = kr0 - qr + NAT_KH - 1
                if v0 and v1:
                    tbl[ty, a, kp] = a0
                elif v1:
                    tbl[ty, a, kp] = 14 + a0 + 1
                elif v0:
                    tbl[ty, a, kp] = 22 + a0 - (NAT_KH - 1)
                else:
                    tbl[ty, a, kp] = N_BIAS_BLK - 1
    return tbl


_NAT_TBL = _nat_block_table()


def _nat_tile(win, idx_fn, q_ref, kb, vb, ck, cv, pb_ref, o_ref, s_scr, c_scr, e_scr, ec_scr):
    lane = lax.broadcasted_iota(jnp.int32, (1, 2 * HD), 1)
    first = lane < HD
    q = q_ref[...]
    w = 2 * GRID_W
    outs = []
    for half in range(2):
        qm = jnp.where(first if half == 0 else jnp.logical_not(first), q, jnp.zeros_like(q))
        s_scr[...] = _dot_nt(qm, kb)
        c_scr[...] = _dot_nt(qm, ck)
        dens = []
        for qr in range(QROWS):
            rows = slice(qr * GRID_W, (qr + 1) * GRID_W)
            lo, hi = win[qr]
            blocks = [s_scr[rows, kp * w:(kp + 1) * w] + pb_ref[half, idx_fn(qr, kp)]
                      for kp in range(lo, hi)]
            c0, c1 = c_scr[rows, 0:w], c_scr[rows, w:2 * w]
            mm = jnp.maximum(c0, c1)
            for b in blocks:
                mm = jnp.maximum(mm, b)
            m = mm.max(axis=-1, keepdims=True)
            e0, e1 = jnp.exp(c0 - m), jnp.exp(c1 - m)
            ec_scr[rows, 0:w] = e0.astype(BF16)
            ec_scr[rows, w:2 * w] = e1.astype(BF16)
            acc = e0 + e1
            for kp in range(KPAIRS):
                if lo <= kp < hi:
                    e = jnp.exp(blocks[kp - lo] - m)
                    acc = acc + e
                    e_scr[rows, kp * w:(kp + 1) * w] = e.astype(BF16)
                else:
                    e_scr[rows, kp * w:(kp + 1) * w] = jnp.zeros((GRID_W, w), BF16)
            dens.append(acc.sum(axis=-1, keepdims=True))
        o = _dot(e_scr[...], vb) + _dot(ec_scr[...], cv)
        outs.append(o / jnp.concatenate(dens, axis=0))
    o_ref[...] = jnp.where(first, outs[0], outs[1]).astype(o_ref.dtype)


def _nat_kernel(tbl, q_ref, k_ref, v_ref, ck_ref, cv_ref, pb_ref, o_ref, s_scr, c_scr, e_scr, ec_scr):
    j = pl.program_id(2)
    nj = pl.num_programs(2)
    start = pl.multiple_of(jnp.clip(j * TQ - BAND // 4, 0, TLAT - BAND), 256)
    kb = k_ref[pl.ds(start, BAND), :]
    vb = v_ref[pl.ds(start, BAND), :]
    args = (q_ref, kb, vb, ck_ref[...], cv_ref[...], pb_ref, o_ref, s_scr, c_scr, e_scr, ec_scr)
    edge = jnp.logical_or(j == 0, j == nj - 1)

    @pl.when(edge)
    def _():
        ty = jnp.where(j == 0, 0, 2)
        full = [(0, KPAIRS)] * QROWS
        _nat_tile(full, lambda qr, kp: tbl[(ty * QROWS + qr) * KPAIRS + kp], *args)

    @pl.when(jnp.logical_not(edge))
    def _():
        win = []
        for qr in range(QROWS):
            live = [kp for kp in range(KPAIRS) if _NAT_TBL[1, qr, kp] != N_BIAS_BLK - 1]
            win.append((live[0], live[-1] + 1))
        _nat_tile(win, lambda qr, kp: int(_NAT_TBL[1, qr, kp]), *args)


def _nat_attn(li, pn, ck, cv, pb, n_ctx_rows, n_lat):
    nj = TLAT // TQ
    qb0 = n_ctx_rows // TQ
    sb0 = n_ctx_rows // TLAT
    npair = NH // 2
    grid_spec = pltpu.PrefetchScalarGridSpec(
        num_scalar_prefetch=1,
        grid=(n_lat, npair, nj),
        in_specs=[
            pl.BlockSpec((TQ, 2 * HD), lambda b, p, j, t: (qb0 + b * nj + j, p)),
            pl.BlockSpec((TLAT, 2 * HD), lambda b, p, j, t: (sb0 + b, npair + p)),
            pl.BlockSpec((TLAT, 2 * HD), lambda b, p, j, t: (sb0 + b, 2 * npair + p)),
            pl.BlockSpec((None, None, SEQ, 2 * HD), lambda b, p, j, t: (li, b, 0, p)),
            pl.BlockSpec((None, None, SEQ, 2 * HD), lambda b, p, j, t: (li, b, 0, p)),
            pl.BlockSpec((None, 2, N_BIAS_BLK, GRID_W, 2 * GRID_W), lambda b, p, j, t: (li, p, 0, 0, 0)),
        ],
        out_specs=pl.BlockSpec((TQ, 2 * HD), lambda b, p, j, t: (b * nj + j, p)),
        scratch_shapes=[
            pltpu.VMEM((TQ, BAND), F32), pltpu.VMEM((TQ, SEQ), F32),
            pltpu.VMEM((TQ, BAND), BF16), pltpu.VMEM((TQ, SEQ), BF16),
        ],
    )
    return pl.pallas_call(
        _nat_kernel,
        grid_spec=grid_spec,
        out_shape=jax.ShapeDtypeStruct((n_lat * TLAT, NW), BF16),
        compiler_params=_cparams(("arbitrary", "arbitrary", "arbitrary")),
    )(jnp.asarray(_NAT_TBL.reshape(-1)), pn, pn, pn, ck, cv, pb)


def _proj_out_kernel(mrow, isctx, cidx, lidx, hprev, hnext, pblk, nblk,
                     x_ref, mod_ref, pc_ref, pcp_ref, pcn_ref, cw_ref, of_ref, ob_ref, gr_ref,
                     gng_ref, hm_ref, yc_ref, yl_ref, w_ref, o_ref):
    del mrow, cidx, lidx, pblk, nblk
    t = pl.program_id(0)
    pc = pc_ref[...].astype(F32)
    u = pc[:, CONV_W:2 * CONV_W] * pc[:, 2 * CONV_W:3 * CONV_W]
    pp = pcp_ref[7:8, :].astype(F32)
    pn = pcn_ref[0:1, :].astype(F32)
    u_prev_edge = pp[:, CONV_W:2 * CONV_W] * pp[:, 2 * CONV_W:3 * CONV_W] * hprev[t].astype(F32)
    u_next_edge = pn[:, CONV_W:2 * CONV_W] * pn[:, 2 * CONV_W:3 * CONV_W] * hnext[t].astype(F32)
    rows = lax.broadcasted_iota(jnp.int32, (TM_OUT, CONV_W), 0)
    u_prev = jnp.where(rows == 0, u_prev_edge, pltpu.roll(u, 1, 0))
    u_next = jnp.where(rows == TM_OUT - 1, u_next_edge, pltpu.roll(u, TM_OUT - 1, 0))
    ctx = isctx[t] == 1
    in_seq = rows & (SEQ - 1)
    u_prev = jnp.where(jnp.logical_and(ctx, in_seq == 0), 0.0, u_prev)
    u_next = jnp.where(jnp.logical_and(ctx, in_seq == SEQ - 1), 0.0, u_next)
    y_conv = pc[:, 0:CONV_W] * (cw_ref[0:1, :] * u_prev + cw_ref[1:2, :] * u + cw_ref[2:3, :] * u_next)

    o = of_ref[...].astype(F32) + ob_ref[...].astype(F32)
    on = o * lax.rsqrt(_head_mean(o * o, hm_ref[...]) + EPS) * gng_ref[...]
    y_gla = on * _silu(gr_ref[...].astype(F32))

    y_nat = jnp.where(isctx[t] == 1, yc_ref[...], yl_ref[...])

    y = (_dot(y_conv.astype(BF16), w_ref[0:CONV_W, :])
         + _dot(y_gla.astype(BF16), w_ref[CONV_W:CONV_W + GVW, :])
         + _dot(y_nat, w_ref[CONV_W + GVW:D, :]))
    o_ref[...] = x_ref[...] + mod_ref[2:3, :] * y


def _proj_out_plan(n_ctx, n_lat):
    nb = TLAT // TM_OUT
    nct = n_ctx * SEQ // TM_OUT
    nt = nct + n_lat * nb
    mrow, isctx, cidx, lidx, hprev, hnext, pblk, nblk = ([] for _ in range(8))
    per = TM_OUT // 8
    for t in range(nt):
        ctx = t < nct
        tl = t - nct
        mrow.append(0 if ctx else 1 + tl // nb)
        isctx.append(1 if ctx else 0)
        cidx.append(min(t, nct - 1))
        lidx.append(max(tl, 0))
        hprev.append(0 if ctx or tl % nb == 0 else 1)
        hnext.append(0 if ctx or tl % nb == nb - 1 else 1)
        pblk.append(max(t * per - 1, 0))
        nblk.append(min((t + 1) * per, nt * per - 1))
    return [np.asarray(a, np.int32) for a in (mrow, isctx, cidx, lidx, hprev, hnext, pblk, nblk)]


def _proj_out(li, x, mod, pc, conv_w, o_f, o_b, pv, gng, hm, y_ctx, y_lat, w_out, plan):
    r = x.shape[0]
    nt = plan[0].shape[0]
    row = lambda t, *_: (t, 0)
    const = lambda t, *_: (0, 0)
    layer = lambda t, *_: (li, 0, 0)
    grid_spec = pltpu.PrefetchScalarGridSpec(
        num_scalar_prefetch=8,
        grid=(nt,),
        in_specs=[
            pl.BlockSpec((TM_OUT, D), row),
            pl.BlockSpec((None, None, 6, D), lambda t, m, *_: (li, m[t], 0, 0)),
            pl.BlockSpec((TM_OUT, PC_W), row),
            pl.BlockSpec((8, PC_W), lambda t, m, ic, ci, lidx, hp, hn, pb, nb_: (pb[t], 0)),
            pl.BlockSpec((8, PC_W), lambda t, m, ic, ci, lidx, hp, hn, pb, nb_: (nb_[t], 0)),
            pl.BlockSpec((None, 3, CONV_W), layer),
            pl.BlockSpec((TM_OUT, GVW), row),
            pl.BlockSpec((TM_OUT, GVW), row),
            pl.BlockSpec((TM_OUT, GVW), lambda t, *_: (t, 1)),
            pl.BlockSpec((None, 1, GVW), layer),
            pl.BlockSpec((GVW, GVW), const),
            pl.BlockSpec((TM_OUT, NW), lambda t, m, ic, ci, *_: (ci[t], 0)),
            pl.BlockSpec((TM_OUT, NW), lambda t, m, ic, ci, lidx, *_: (lidx[t], 0)),
            pl.BlockSpec((None, D, D), layer),
        ],
        out_specs=pl.BlockSpec((TM_OUT, D), row),
    )
    return pl.pallas_call(
        _proj_out_kernel,
        grid_spec=grid_spec,
        out_shape=jax.ShapeDtypeStruct((r, D), F32),
        compiler_params=_cparams(("arbitrary",)),
    )(*plan, x, mod, pc, pc, pc, conv_w, o_f, o_b, pv, gng, hm, y_ctx, y_lat, w_out)


def _ffn_dense_kernel(mrow, x_ref, g_ref, mod_ref, wg_ref, wu_ref, wd_ref, o_ref, h_scr, acc):
    del mrow
    f = pl.program_id(1)

    @pl.when(f == 0)
    def _():
        h_scr[...] = _norm_mod(x_ref[...], g_ref[...], mod_ref[3:4, :], mod_ref[4:5, :]).astype(BF16)
        acc[...] = jnp.zeros_like(acc)

    h = h_scr[...]
    hid = _silu(_dot(h, wg_ref[...])) * _dot(h, wu_ref[...])
    acc[...] += _dot(hid.astype(BF16), wd_ref[...])

    @pl.when(f == pl.num_programs(1) - 1)
    def _():
        o_ref[...] = x_ref[...] + mod_ref[5:6, :] * acc[...]


def _ffn_dense(li, j, tm, tf, x, g2, mod, wg, wu, wd, mrow):
    r = x.shape[0]
    nf = D_FF // tf
    grid_spec = pltpu.PrefetchScalarGridSpec(
        num_scalar_prefetch=1,
        grid=(r // tm, nf),
        in_specs=[
            pl.BlockSpec((tm, D), lambda t, f, m: (t, 0)),
            pl.BlockSpec((None, 1, D), lambda t, f, m: (li, 0, 0)),
            pl.BlockSpec((None, None, 6, D), lambda t, f, m: (li, m[t], 0, 0)),
            pl.BlockSpec((None, D, tf), lambda t, f, m: (j, 0, f)),
            pl.BlockSpec((None, D, tf), lambda t, f, m: (j, 0, f)),
            pl.BlockSpec((None, tf, D), lambda t, f, m: (j, f, 0)),
        ],
        out_specs=pl.BlockSpec((tm, D), lambda t, f, m: (t, 0)),
        scratch_shapes=[pltpu.VMEM((tm, D), BF16), pltpu.VMEM((tm, D), F32)],
    )
    return pl.pallas_call(
        _ffn_dense_kernel,
        grid_spec=grid_spec,
        out_shape=jax.ShapeDtypeStruct((r, D), F32),
        compiler_params=_cparams(("arbitrary", "arbitrary")),
    )(mrow, x, g2, mod, wg, wu, wd)


def _router_kernel(mrow, x_ref, g_ref, mod_ref, rhi_ref, rlo_ref, h_ref, ri_ref, rw_ref):
    del mrow
    h = _norm_mod(x_ref[...], g_ref[...], mod_ref[3:4, :], mod_ref[4:5, :])
    h_ref[...] = h
    hhi, hlo = _split_bf16(h)
    logits = _dot(hhi, rhi_ref[...]) + _dot(hlo, rhi_ref[...]) + _dot(hhi, rlo_ref[...])
    lane = lax.broadcasted_iota(jnp.int32, logits.shape, 1).astype(F32)
    lg = jnp.where(lane < N_EXP, logits, -jnp.inf)
    m1 = lg.max(axis=-1, keepdims=True)
    i1 = jnp.where(lg == m1, lane, 128.0).min(axis=-1, keepdims=True)
    lg2 = jnp.where(lane == i1, -jnp.inf, lg)
    m2 = lg2.max(axis=-1, keepdims=True)
    i2 = jnp.where(lg2 == m2, lane, 128.0).min(axis=-1, keepdims=True)
    e = jnp.exp(m2 - m1)
    w1 = 1.0 / (1.0 + e)
    w2 = e / (1.0 + e)
    ri_ref[...] = jnp.where(lane == 0.0, i1, jnp.where(lane == 1.0, i2, 0.0)).astype(jnp.int32)
    rw_ref[...] = jnp.where(lane == 0.0, w1, jnp.where(lane == 1.0, w2, 0.0))


def _router(li, j, x, g2, mod, rhi, rlo, mrow):
    r = x.shape[0]
    row = lambda t, m: (t, 0)
    grid_spec = pltpu.PrefetchScalarGridSpec(
        num_scalar_prefetch=1,
        grid=(r // TM_PROJ,),
        in_specs=[
            pl.BlockSpec((TM_PROJ, D), row),
            pl.BlockSpec((None, 1, D), lambda t, m: (li, 0, 0)),
            pl.BlockSpec((None, None, 6, D), lambda t, m: (li, m[t], 0, 0)),
            pl.BlockSpec((None, D, 128), lambda t, m: (j, 0, 0)),
            pl.BlockSpec((None, D, 128), lambda t, m: (j, 0, 0)),
        ],
        out_specs=[
            pl.BlockSpec((TM_PROJ, D), row),
            pl.BlockSpec((TM_PROJ, 128), row),
            pl.BlockSpec((TM_PROJ, 128), row),
        ],
    )
    return pl.pallas_call(
        _router_kernel,
        grid_spec=grid_spec,
        out_shape=[
            jax.ShapeDtypeStruct((r, D), F32),
            jax.ShapeDtypeStruct((r, 128), jnp.int32),
            jax.ShapeDtypeStruct((r, 128), F32),
        ],
        compiler_params=_cparams(("arbitrary",)),
    )(mrow, x, g2, mod, rhi, rlo)


def _row_copy(src_ref, src_row, dst_ref, dst_row, sem):
    return pltpu.make_async_copy(src_ref.at[pl.ds(src_row, 1)], dst_ref.at[pl.ds(dst_row, 1)], sem)


def _dispatch_kernel(dest, pstart, plen, nused, h_ref, xg_hbm, zbuf, sem, zsem):
    t = pl.program_id(0)
    base = t * 2 * TD_DISP
    tm = zbuf.shape[0]
    n_tiles = xg_hbm.shape[0] // tm

    def start(rr, c):
        _row_copy(h_ref, rr, xg_hbm, dest[base + 2 * rr], sem).start()
        _row_copy(h_ref, rr, xg_hbm, dest[base + 2 * rr + 1], sem).start()
        return c

    lax.fori_loop(0, TD_DISP, start, 0, unroll=8)

    @pl.when(t == 0)
    def _():
        zbuf[...] = jnp.zeros_like(zbuf)
        for e in range(N_EXP):
            n = plen[e]
            s0 = pstart[e]
            head = n & 7
            pads = [(rr < head, _row_copy(zbuf, 0, xg_hbm, s0 + rr, zsem)) for rr in range(7)]
            for bit in reversed(range(3, tm.bit_length() - 1)):
                size = 1 << bit
                first = pl.multiple_of(s0 + head + ((n - head) & ~(2 * size - 1)), 8)
                pads.append(((n & size) != 0,
                             pltpu.make_async_copy(zbuf.at[pl.ds(0, size)], xg_hbm.at[pl.ds(first, size)], zsem)))
            for take, cp in pads:
                pl.when(take)(cp.start)
            for take, cp in pads:
                pl.when(take)(cp.wait)

        def tile_copy(i):
            return pltpu.make_async_copy(zbuf, xg_hbm.at[pl.ds(i * tm, tm)], zsem)

        def tstart(i, c):
            tile_copy(i).start()
            return c

        def twait(i, c):
            tile_copy(i).wait()
            return c

        lax.fori_loop(nused[0], n_tiles, tstart, 0)
        lax.fori_loop(nused[0], n_tiles, twait, 0)

    def wait(rr, c):
        _row_copy(h_ref, 0, xg_hbm, 0, sem).wait()
        return c

    lax.fori_loop(0, 2 * TD_DISP, wait, 0, unroll=8)


def _dispatch(tm, h, dest, pstart, plen, nused, n_tiles):
    r = h.shape[0]
    grid_spec = pltpu.PrefetchScalarGridSpec(
        num_scalar_prefetch=4,
        grid=(r // TD_DISP,),
        in_specs=[pl.BlockSpec((TD_DISP, D), lambda t, *_: (t, 0))],
        out_specs=pl.BlockSpec(memory_space=pl.ANY),
        scratch_shapes=[pltpu.VMEM((tm, D), F32), pltpu.SemaphoreType.DMA(()),
                        pltpu.SemaphoreType.DMA(())],
    )
    return pl.pallas_call(
        _dispatch_kernel,
        grid_spec=grid_spec,
        out_shape=jax.ShapeDtypeStruct((n_tiles * tm, D), F32),
        compiler_params=_cparams(("arbitrary",)),
    )(dest, pstart, plen, nused, h)


def _ffn_grouped_kernel(te, nused, x_ref, wg_ref, wu_ref, wd_ref, o_ref, xb, acc):
    del te
    i = pl.program_id(0)
    f = pl.program_id(1)
    last = pl.num_programs(1) - 1
    used = i < nused[0]

    @pl.when(jnp.logical_and(used, f == 0))
    def _():
        xb[...] = x_ref[...].astype(BF16)
        acc[...] = jnp.zeros_like(acc)

    @pl.when(used)
    def _():
        h = xb[...]
        hid = _silu(_dot(h, wg_ref[...].astype(BF16))) * _dot(h, wu_ref[...].astype(BF16))
        acc[...] += _dot(hid.astype(BF16), wd_ref[...].astype(BF16))

    @pl.when(jnp.logical_and(used, f == last))
    def _():
        o_ref[...] = acc[...]

    @pl.when(jnp.logical_and(jnp.logical_not(used), f == last))
    def _():
        o_ref[...] = jnp.zeros_like(o_ref)


def _ffn_grouped(j, tm, tf, xg, wg, wu, wd, te, nused, n_tiles):
    nf = D_EXP // tf
    fidx = lambda i, f, n: jnp.where(i < n[0], f, nf - 1)
    grid_spec = pltpu.PrefetchScalarGridSpec(
        num_scalar_prefetch=2,
        grid=(n_tiles, nf),
        in_specs=[
            pl.BlockSpec((tm, D), lambda i, f, e, n: (jnp.minimum(i, n[0] - 1), 0)),
            pl.BlockSpec((None, None, D, tf), lambda i, f, e, n: (j, e[i], 0, fidx(i, f, n))),
            pl.BlockSpec((None, None, D, tf), lambda i, f, e, n: (j, e[i], 0, fidx(i, f, n))),
            pl.BlockSpec((None, None, tf, D), lambda i, f, e, n: (j, e[i], fidx(i, f, n), 0)),
        ],
        out_specs=pl.BlockSpec((tm, D), lambda i, f, e, n: (i, 0)),
        scratch_shapes=[pltpu.VMEM((tm, D), BF16), pltpu.VMEM((tm, D), F32)],
    )
    return pl.pallas_call(
        _ffn_grouped_kernel,
        grid_spec=grid_spec,
        out_shape=jax.ShapeDtypeStruct((n_tiles * tm, D), F32),
        compiler_params=_cparams(("arbitrary", "arbitrary")),
    )(te, nused, xg, wg, wu, wd)


def _combine_kernel(mrow, dest, x_ref, mod_ref, rw_ref, y_hbm, o_ref, buf, sem):
    del mrow
    t = pl.program_id(0)
    base = t * TC_COMB * 2

    def start(rr, c):
        _row_copy(y_hbm, dest[base + rr], buf, rr, sem).start()
        return c

    lax.fori_loop(0, 2 * TC_COMB, start, 0, unroll=8)

    def wait(rr, c):
        _row_copy(y_hbm, 0, buf, 0, sem).wait()
        return c

    lax.fori_loop(0, 2 * TC_COMB, wait, 0, unroll=8)
    rw = rw_ref[...]
    y = rw[:, 0:1] * buf[0:TC_COMB, :] + rw[:, 1:2] * buf[TC_COMB:2 * TC_COMB, :]
    o_ref[...] = x_ref[...] + mod_ref[5:6, :] * y


def _combine(li, x, mod, rw, yg, dest, mrow):
    r = x.shape[0]
    grid_spec = pltpu.PrefetchScalarGridSpec(
        num_scalar_prefetch=2,
        grid=(r // TC_COMB,),
        in_specs=[
            pl.BlockSpec((TC_COMB, D), lambda t, m, d: (t, 0)),
            pl.BlockSpec((None, None, 6, D), lambda t, m, d: (li, m[t], 0, 0)),
            pl.BlockSpec((TC_COMB, 128), lambda t, m, d: (t, 0)),
            pl.BlockSpec(memory_space=pl.ANY),
        ],
        out_specs=pl.BlockSpec((TC_COMB, D), lambda t, m, d: (t, 0)),
        scratch_shapes=[pltpu.VMEM((2 * TC_COMB, D), F32), pltpu.SemaphoreType.DMA(())],
    )
    return pl.pallas_call(
        _combine_kernel,
        grid_spec=grid_spec,
        out_shape=jax.ShapeDtypeStruct((r, D), F32),
        compiler_params=_cparams(("arbitrary",)),
    )(mrow, dest, x, mod, rw, yg)


def _moe_plan(ri, tm, n_tiles):
    r = ri.shape[0]
    ef = ri[:, :2].reshape(-1)
    oh = (ef[:, None] == jnp.arange(N_EXP, dtype=jnp.int32)[None, :]).astype(jnp.int32)
    csum = jnp.cumsum(oh, axis=0)
    pos = jnp.sum(csum * oh, axis=1) - 1
    counts = csum[-1]
    tiles = (counts + tm - 1) // tm
    tile_end = jnp.cumsum(tiles)
    off = (tile_end - tiles) * tm
    dest = (jnp.sum(off[None, :] * oh, axis=1) + pos).astype(jnp.int32)
    tile_id = jnp.arange(n_tiles, dtype=jnp.int32)
    te = jnp.minimum(jnp.sum((tile_end[None, :] <= tile_id[:, None]).astype(jnp.int32), axis=1), N_EXP - 1)
    nused = tile_end[-1:].astype(jnp.int32)
    pstart = (off + counts).astype(jnp.int32)
    plen = (tiles * tm - counts).astype(jnp.int32)
    dest_t = dest.reshape(r // TC_COMB, TC_COMB, 2).transpose(0, 2, 1).reshape(-1)
    return dest, dest_t, te.astype(jnp.int32), nused, pstart, plen


def _mod_rows(n_ctx, n_lat, tile):
    rows = [0] * (n_ctx * SEQ // tile)
    for b in range(n_lat):
        rows += [1 + b] * (TLAT // tile)
    return np.asarray(rows, np.int32)


def _rope_tables():
    t = np.arange(TLAT)
    lane = np.arange(GWP)
    p = lane % GK
    sub = p % (GK // 2)
    nf = GK // 4
    freq = ROPE_BASE ** (-(sub % nf).astype(np.float32) / nf)
    pos = np.where((p < GK // 2)[None, :], (t // GRID_W)[:, None], (t % GRID_W)[:, None]).astype(np.float32)
    ang = jnp.asarray(pos) * jnp.asarray(freq.astype(np.float32))[None, :]
    cos, sin = jnp.cos(ang), jnp.sin(ang)
    lowhalf = jnp.asarray((sub < nf)[None, :])
    sina = jnp.where(lowhalf, -sin, 0.0)
    sinb = jnp.where(lowhalf, 0.0, sin)
    pad1 = jnp.ones((TM_PROJ, GWP), F32)
    pad0 = jnp.zeros((TM_PROJ, GWP), F32)
    return (jnp.concatenate([cos, pad1]), jnp.concatenate([sina, pad0]), jnp.concatenate([sinb, pad0]))


def _nat_bias_blocks(rpb):
    nl = rpb.shape[0]
    col = np.arange(GRID_W)
    c0 = np.clip(col - NAT_KW // 2, 0, GRID_W - NAT_KW)
    in_win = (col[None, :] >= c0[:, None]) & (col[None, :] < c0[:, None] + NAT_KW)
    dc = np.clip(col[None, :] - col[:, None], -(NAT_KW - 1), NAT_KW - 1) + NAT_KW - 1
    cm = jnp.where(jnp.asarray(in_win)[None, None, None], rpb[:, :, :, dc], NEG)
    na = 2 * NAT_KH - 1
    neg = jnp.full((nl, NH, NAT_KH, GRID_W, GRID_W), NEG, F32)
    full = jnp.concatenate([cm[:, :, 0:na - 1], cm[:, :, 1:na]], axis=-1)
    left = jnp.concatenate([neg, cm[:, :, 0:NAT_KH]], axis=-1)
    right = jnp.concatenate([cm[:, :, NAT_KH - 1:na], neg], axis=-1)
    none = jnp.concatenate([neg[:, :, 0:1], neg[:, :, 0:1]], axis=-1)
    return jnp.concatenate([full, left, right, none], axis=2)


def _heads_to_rows(a):
    b, h, t, d = a.shape
    return a.transpose(0, 2, 1, 3).reshape(b, t, h * d)


def _rows_to_heads(a, b, t):
    return a.reshape(b, t, NH, HD).transpose(0, 2, 1, 3)


def _state_to_blockdiag(s):
    b = s.shape[0]
    eye = jnp.eye(GH, dtype=s.dtype)
    bd = jnp.einsum("bhkv,hg->bhvgk", s, eye).reshape(b, GVW, GW)
    return jnp.pad(bd, ((0, 0), (0, 0), (0, GWP - GW)))


def _blockdiag_to_state(st):
    b = st.shape[0]
    s5 = st[:, :, :GW].reshape(b, GH, GV, GH, GK)
    return jnp.stack([s5[:, h, :, h, :] for h in range(GH)], axis=1).transpose(0, 1, 3, 2)


def kernel(x_prompt, x_sample, cache_nat_k, cache_nat_v, state_gla, c, c_ctx, norm1_g, norm2_g, w_mod, b_mod, w_in, conv_w, gla_gate_w2, gla_gate_b, gla_norm_g, nat_q_norm_g, nat_k_norm_g, nat_rpb, w_out, ffn_w_gate, ffn_w_up, ffn_w_down, moe_router, moe_w_gate, moe_w_up, moe_w_down):
    n_ctx, n_lat = x_prompt.shape[0], x_sample.shape[0]
    depth = w_in.shape[0]
    ncr = n_ctx * SEQ
    r = ncr + n_lat * TLAT
    assert ncr % TLAT == 0 and n_lat + 1 <= 8

    x = jnp.concatenate([x_prompt.reshape(ncr, D), x_sample.reshape(n_lat * TLAT, D)], axis=0)
    cvecs = jnp.zeros((8, D), F32).at[0].set(c_ctx).at[1:1 + n_lat].set(c)
    mod = _modulation(cvecs, w_mod, b_mod)

    z = lambda n: jnp.zeros((depth, D, n), F32)
    w_in_p = jnp.concatenate([
        w_in[:, :, 768:960], z(GWP - GW), w_in[:, :, 960:1152], z(GWP - GW),
        w_in[:, :, 1920:1952], z(128 - 2 * LR),
        w_in[:, :, 0:768], w_in[:, :, 1152:1920], w_in[:, :, 1952:3104]], axis=-1).astype(BF16)
    w_out_b = w_out.astype(BF16)
    w2 = jnp.pad(gla_gate_w2, ((0, 0), (0, 0), (0, 0), (0, GWP - GW)))
    w2f = jnp.pad(w2[:, 0], ((0, 0), (0, 128 - LR), (0, 0))).astype(BF16)
    w2b = jnp.pad(w2[:, 1], ((0, 0), (LR, 128 - 2 * LR), (0, 0))).astype(BF16)
    gb = jnp.pad(gla_gate_b, ((0, 0), (0, 0), (0, GWP - GW)))[:, :, None, :]
    hm = jnp.asarray(np.kron(np.eye(NH), np.full((HD, HD), 1.0 / HD)), BF16)
    g1 = norm1_g[:, None, :]
    g2 = norm2_g[:, None, :]
    qg = jnp.tile(nat_q_norm_g, (1, NH))[:, None, :]
    kg = jnp.tile(nat_k_norm_g, (1, NH))[:, None, :]
    gng = jnp.tile(gla_norm_g, (1, GH))[:, None, :]
    gla_masks = _gla_masks()
    rope = _rope_tables()
    pb = _nat_bias_blocks(nat_rpb)
    ck = _heads_to_rows(cache_nat_k.transpose(1, 0, 2, 3, 4).reshape(depth * n_lat, NH, SEQ, HD))
    cv = _heads_to_rows(cache_nat_v.transpose(1, 0, 2, 3, 4).reshape(depth * n_lat, NH, SEQ, HD))
    ck = ck.reshape(depth, n_lat, SEQ, NW).astype(BF16)
    cv = cv.reshape(depth, n_lat, SEQ, NW).astype(BF16)
    zero_state = jnp.zeros((n_ctx, GVW, GWP), F32)
    ffn_g, ffn_u, ffn_d = ffn_w_gate.astype(BF16), ffn_w_up.astype(BF16), ffn_w_down.astype(BF16)
    moe_w = []
    for jj in range(moe_w_gate.shape[0]):
        if MOE_TILES[jj % len(MOE_TILES)][2]:
            moe_w.append((0, moe_w_gate[jj:jj + 1].astype(BF16), moe_w_up[jj:jj + 1].astype(BF16),
                          moe_w_down[jj:jj + 1].astype(BF16)))
        else:
            moe_w.append((jj, moe_w_gate, moe_w_up, moe_w_down))
    router_p = jnp.pad(moe_router, ((0, 0), (0, 0), (0, 128 - N_EXP)))
    rhi = router_p.astype(BF16)
    rlo = (router_p - rhi.astype(F32)).astype(BF16)

    mrow_proj = jnp.asarray(_mod_rows(n_ctx, n_lat, TM_PROJ))
    in_plan = [jnp.asarray(a) for a in _proj_in_plan(n_ctx, n_lat)]
    mrow_comb = jnp.asarray(_mod_rows(n_ctx, n_lat, TC_COMB))
    gla_plan = [jnp.asarray(a) for a in _gla_plan(n_ctx, n_lat)]
    out_plan = [jnp.asarray(a) for a in _proj_out_plan(n_ctx, n_lat)]

    new_k, new_v, new_s = [], [], []
    for i in range(depth):
        pg, pc, pv, pn, knf, vnf = _proj_in(i, x, g1, mod, w_in_p, hm, qg, kg, rope, w2f, w2b, gb,
                                            in_plan, ncr)
        s0f = jnp.concatenate([zero_state, _state_to_blockdiag(state_gla[:, i, 0])], axis=0)
        s0b = jnp.concatenate([zero_state, _state_to_blockdiag(state_gla[:, i, 1])], axis=0)
        o_f, o_b, sff, sfb = _gla(pg, pv, gla_masks, s0f, s0b, gla_plan)
        y_ctx = _ctx_attn(pn, n_ctx)
        y_lat = _nat_attn(i, pn, ck, cv, pb, ncr, n_lat)
        x = _proj_out(i, x, mod, pc, conv_w, o_f, o_b, pv, gng, hm, y_ctx, y_lat, w_out_b, out_plan)
        j = i // 2
        if i % 2 == 0:
            tm, tf = FFN_TILE
            mrow_ffn = jnp.asarray(_mod_rows(n_ctx, n_lat, tm))
            x = _ffn_dense(i, j, tm, tf, x, g2, mod, ffn_g, ffn_u, ffn_d, mrow_ffn)
        else:
            tm, tf, _ = MOE_TILES[j % len(MOE_TILES)]
            wj, moe_g, moe_u, moe_d = moe_w[j]
            n_tiles = 2 * r // tm + N_EXP
            h, ri, rw = _router(i, j, x, g2, mod, rhi, rlo, mrow_proj)
            dest, dest_t, te, nused, pstart, plen = _moe_plan(ri, tm, n_tiles)
            xg = _dispatch(tm, h, dest, pstart, plen, nused, n_tiles)
            yg = _ffn_grouped(wj, tm, tf, xg, moe_g, moe_u, moe_d, te, nused, n_tiles)
            x = _combine(i, x, mod, rw, yg, dest_t, mrow_comb)
        new_k.append(_rows_to_heads(knf, n_ctx, SEQ))
        new_v.append(_rows_to_heads(vnf, n_ctx, SEQ))
        new_s.append(jnp.stack([_blockdiag_to_state(sff[:n_ctx]), _blockdiag_to_state(sfb[:n_ctx])], axis=1))

    y_prompt = x[:ncr].reshape(n_ctx, SEQ, D)
    y_sample = x[ncr:].reshape(n_lat, TLAT, D)
    return (y_prompt, y_sample, jnp.stack(new_k, axis=1), jnp.stack(new_v, axis=1), jnp.stack(new_s, axis=1))
```

```python
import numpy as np
import jax
import jax.numpy as jnp
from jax import lax
from jax.experimental import pallas as pl
from jax.experimental.pallas import tpu as pltpu

F32 = jnp.float32
BF16 = jnp.bfloat16

D = 1024
SEQ = 256
TLAT = 4096
GRID_W = 64
HD = 64
CONV_W = 256
GH, GK, GV = 6, 32, 64
NH = 6
LR = 16
GATE_NORM = 16.0
CHUNK = 64
NAT_KH, NAT_KW = 8, 16
ROPE_BASE = 10000.0
D_FF = 2816
N_EXP = 8
D_EXP = 3584
EPS = 1e-6
NEG = -1e30

GW = GH * GK
GWP = 256
GVW = GH * GV
NW = NH * HD

PG_IN = 640
PG_W = 1024
PC_W = 768
PV_W = 768
PN_W = 1152
W_IN_P = PG_IN + PC_W + PV_W + PN_W

TM_PROJ = 512
BLK = 256
TM_OUT = 512
TQ = 512
BAND = 1024
FFN_TILE = (512, 1408)
MOE_TILE = (1024, 512)
TC_COMB = 256
TD_DISP = 256

VMEM_LIMIT = 56 * 1024 * 1024


def _cparams(sem):
    return pltpu.CompilerParams(dimension_semantics=sem, vmem_limit_bytes=VMEM_LIMIT)


def _dot(a, b):
    return jnp.dot(a, b, preferred_element_type=F32)


def _dot_nt(a, b):
    return lax.dot_general(a, b, (((1,), (1,)), ((), ())), preferred_element_type=F32)


def _dot_tn(a, b):
    return lax.dot_general(a, b, (((0,), (0,)), ((), ())), preferred_element_type=F32)


def _split_bf16(a):
    hi = a.astype(BF16)
    lo = (a - hi.astype(F32)).astype(BF16)
    return hi, lo


def _silu(a):
    return a * jax.nn.sigmoid(a)


def _head_mean(sq, hm):
    hi, lo = _split_bf16(sq)
    return _dot(hi, hm) + _dot(lo, hm)


def _mod_kernel(c_ref, w_ref, b_ref, o_ref):
    s = _silu(c_ref[...])
    o_ref[...] = _dot(s.astype(BF16), w_ref[...].astype(BF16)) + b_ref[...]


def _modulation(cvecs, w_mod, b_mod):
    nl = w_mod.shape[0]
    out = pl.pallas_call(
        _mod_kernel,
        grid=(nl, 6),
        in_specs=[
            pl.BlockSpec((8, D), lambda l, j: (0, 0)),
            pl.BlockSpec((None, D, D), lambda l, j: (l, 0, j)),
            pl.BlockSpec((None, 1, D), lambda l, j: (l, 0, j)),
        ],
        out_specs=pl.BlockSpec((None, 8, D), lambda l, j: (l, 0, j)),
        out_shape=jax.ShapeDtypeStruct((nl, 8, 6 * D), F32),
        compiler_params=_cparams(("arbitrary", "arbitrary")),
    )(cvecs, w_mod, b_mod.reshape(nl, 1, 6 * D))
    return out.reshape(nl, 8, 6, D)


def _norm_mod(x, g, shift, scale):
    ms = jnp.mean(x * x, axis=-1, keepdims=True)
    return x * lax.rsqrt(ms + EPS) * g * (1.0 + scale) + shift


def _log_sigmoid(z):
    return jnp.minimum(z, 0.0) - jnp.log1p(jnp.exp(-jnp.abs(z)))


def _rope(a, cos, sina, sinb):
    return a * cos + pltpu.roll(a, GWP - 8, 1) * sina + pltpu.roll(a, 8, 1) * sinb


def _proj_in_kernel(rtile, mrow, tbl, kvblk, x_ref, g_ref, mod_ref, w_ref, hm_ref, qg_ref, kg_ref,
                    cos_ref, sina_ref, sinb_ref, w2f_ref, w2b_ref, gbf_ref, gbb_ref,
                    pg_ref, pc_ref, pv_ref, pn_ref, knf_ref, vnf_ref):
    del rtile, mrow, tbl, kvblk
    h = _norm_mod(x_ref[...], g_ref[...], mod_ref[0:1, :], mod_ref[1:2, :]).astype(BF16)
    gq = _dot(h, w_ref[:, 0:GWP]) * (GK ** -0.5)
    gk = _dot(h, w_ref[:, GWP:2 * GWP])
    lr = _dot(h, w_ref[:, 2 * GWP:PG_IN]).astype(BF16)
    cos, sina, sinb = cos_ref[...], sina_ref[...], sinb_ref[...]
    pg_ref[:, 0:GWP] = _rope(gq, cos, sina, sinb).astype(BF16)
    pg_ref[:, GWP:2 * GWP] = _rope(gk, cos, sina, sinb).astype(BF16)
    gf = _log_sigmoid(_dot(lr, w2f_ref[...]) + gbf_ref[...]) * (1.0 / GATE_NORM)
    gb = _log_sigmoid(_dot(lr, w2b_ref[...]) + gbb_ref[...]) * (1.0 / GATE_NORM)
    pg_ref[:, 2 * GWP:3 * GWP] = gf.astype(BF16)
    pg_ref[:, 3 * GWP:4 * GWP] = gb.astype(BF16)
    pc_ref[...] = _dot(h, w_ref[:, PG_IN:PG_IN + PC_W]).astype(BF16)
    pv_ref[...] = _dot(h, w_ref[:, PG_IN + PC_W:PG_IN + PC_W + PV_W]).astype(BF16)
    o = PG_IN + PC_W + PV_W
    nq = _dot(h, w_ref[:, o:o + NW])
    nk = _dot(h, w_ref[:, o + NW:o + 2 * NW])
    nv = _dot(h, w_ref[:, o + 2 * NW:o + 3 * NW])
    hm = hm_ref[...]
    qn = nq * lax.rsqrt(_head_mean(nq * nq, hm) + EPS) * qg_ref[...]
    kn = nk * lax.rsqrt(_head_mean(nk * nk, hm) + EPS) * kg_ref[...]
    pn_ref[:, 0:NW] = (qn * (HD ** -0.5)).astype(BF16)
    pn_ref[:, NW:2 * NW] = kn.astype(BF16)
    pn_ref[:, 2 * NW:3 * NW] = nv.astype(BF16)
    knf_ref[...] = kn
    vnf_ref[...] = nv


def _proj_in_plan(n_ctx, n_lat):
    nct = n_ctx * SEQ // TM_PROJ
    per = TLAT // TM_PROJ
    rtile, mrow, tbl, kvblk = [], [], [], []
    for b in range(n_lat):
        for j in range(per):
            rtile.append(nct + b * per + j); mrow.append(1 + b); tbl.append(j); kvblk.append(0)
    for t in range(nct):
        rtile.append(t); mrow.append(0); tbl.append(per); kvblk.append(t)
    return [np.asarray(a, np.int32) for a in (rtile, mrow, tbl, kvblk)]


def _proj_in(li, x, g1, mod, w_in_p, hm, qg, kg, rope, w2f, w2b, gb, plan, n_ctx_rows):
    r = x.shape[0]
    nt = r // TM_PROJ
    cos, sina, sinb = rope
    row = lambda t, rt, *_: (rt[t], 0)
    const = lambda t, *_: (0, 0)
    layer = lambda t, *_: (li, 0, 0)
    tab = pl.BlockSpec((TM_PROJ, GWP), lambda t, rt, m, tb, kv: (tb[t], 0))
    kvo = pl.BlockSpec((TM_PROJ, NW), lambda t, rt, m, tb, kv: (kv[t], 0))
    grid_spec = pltpu.PrefetchScalarGridSpec(
        num_scalar_prefetch=4,
        grid=(nt,),
        in_specs=[
            pl.BlockSpec((TM_PROJ, D), row),
            pl.BlockSpec((None, 1, D), layer),
            pl.BlockSpec((None, None, 6, D), lambda t, rt, m, *_: (li, m[t], 0, 0)),
            pl.BlockSpec((None, D, W_IN_P), layer),
            pl.BlockSpec((NW, NW), const),
            pl.BlockSpec((None, 1, NW), layer),
            pl.BlockSpec((None, 1, NW), layer),
            tab, tab, tab,
            pl.BlockSpec((None, 128, GWP), layer), pl.BlockSpec((None, 128, GWP), layer),
            pl.BlockSpec((None, None, 1, GWP), lambda t, *_: (li, 0, 0, 0)),
            pl.BlockSpec((None, None, 1, GWP), lambda t, *_: (li, 1, 0, 0)),
        ],
        out_specs=[
            pl.BlockSpec((TM_PROJ, PG_W), row),
            pl.BlockSpec((TM_PROJ, PC_W), row),
            pl.BlockSpec((TM_PROJ, PV_W), row),
            pl.BlockSpec((TM_PROJ, PN_W), row),
            kvo, kvo,
        ],
    )
    return pl.pallas_call(
        _proj_in_kernel,
        grid_spec=grid_spec,
        out_shape=[
            jax.ShapeDtypeStruct((r, PG_W), BF16),
            jax.ShapeDtypeStruct((r, PC_W), BF16),
            jax.ShapeDtypeStruct((r, PV_W), BF16),
            jax.ShapeDtypeStruct((r, PN_W), BF16),
            jax.ShapeDtypeStruct((n_ctx_rows, NW), F32),
            jax.ShapeDtypeStruct((n_ctx_rows, NW), F32),
        ],
        compiler_params=_cparams(("arbitrary",)),
    )(*plan, x, g1, mod, w_in_p, hm, qg, kg, cos, sina, sinb, w2f, w2b, gb, gb)


def _gla_direction(pg, v, tri, km, vm, am, sm, st_ref, o_ref, rev):
    q = pg[:, 0:GWP].astype(F32)
    k = pg[:, GWP:2 * GWP].astype(F32)
    g = pg[:, 3 * GWP:4 * GWP] if rev else pg[:, 2 * GWP:3 * GWP]
    cum = _dot(tri, g)
    qd = (q * jnp.exp(cum)).astype(BF16)
    kd = (k * jnp.exp(-cum)).astype(BF16)
    chunks = range(BLK // CHUNK)
    for c in (reversed(chunks) if rev else chunks):
        lo = c * CHUNK
        sl = slice(lo, lo + CHUNK)
        edge = lo if rev else lo + CHUNK - 1
        cend = cum[edge:edge + 1, :]
        kst = (k[sl] * jnp.exp(cend - cum[sl])).astype(BF16)
        decay = jnp.exp(cend)
        kblk = jnp.concatenate([kd[sl]] * GH, axis=0) * km
        a = jnp.where(am > 0.0, _dot_nt(qd[sl], kblk), 0.0).astype(BF16)
        v_c = v[sl]
        vblk = jnp.concatenate([v_c] * GH, axis=0) * vm
        st = st_ref[...]
        o = _dot(a, vblk) + _dot_nt(qd[sl], st.astype(BF16))
        o_ref[sl, :] = o.astype(o_ref.dtype)
        ut = _dot_tn(v_c, kst)
        st_ref[...] = st * decay + ut * sm


def _gla_kernel(fblk, bblk, sidx, oidx, first, isctx,
                pgf_ref, pvf_ref, pgb_ref, pvb_ref, trif_ref, trib_ref, km_ref, vm_ref,
                amf_ref, amb_ref, sm_ref, s0f_ref, s0b_ref,
                of_ref, ob_ref, sff_ref, sfb_ref, stf, stb):
    del fblk, bblk, sidx, oidx
    u = pl.program_id(0)

    @pl.when(first[u] == 1)
    def _():
        ctx = isctx[u] == 1
        stf[...] = jnp.where(ctx, 0.0, s0f_ref[...])
        stb[...] = jnp.where(ctx, 0.0, s0b_ref[...])

    km, vm, sm = km_ref[...], vm_ref[...], sm_ref[...]
    _gla_direction(pgf_ref[...], pvf_ref[...], trif_ref[...], km, vm, amf_ref[...], sm, stf, of_ref, False)
    _gla_direction(pgb_ref[...], pvb_ref[...], trib_ref[...], km, vm, amb_ref[...], sm, stb, ob_ref, True)
    sff_ref[...] = stf[...]
    sfb_ref[...] = stb[...]


def _gla_plan(n_ctx, n_lat):
    nb = TLAT // BLK
    fblk, bblk, sidx, oidx, first, isctx = [], [], [], [], [], []
    for s in range(n_lat):
        for j in range(nb):
            fblk.append(n_ctx + s * nb + j)
            bblk.append(n_ctx + s * nb + nb - 1 - j)
            sidx.append(s); oidx.append(0); first.append(1 if j == 0 else 0); isctx.append(0)
    for s in range(n_ctx):
        fblk.append(s); bblk.append(s); sidx.append(0); oidx.append(s); first.append(1); isctx.append(1)
    return [np.asarray(a, np.int32) for a in (fblk, bblk, sidx, oidx, first, isctx)]


def _gla_masks():
    ii = np.arange(BLK)
    same = (ii[:, None] // CHUNK) == (ii[None, :] // CHUNK)
    trif = same & (ii[None, :] <= ii[:, None])
    trib = same & (ii[None, :] >= ii[:, None])
    rk = np.arange(GVW)[:, None] // CHUNK
    km = rk == (np.arange(GWP)[None, :] // GK)
    vm = rk == (np.arange(GVW)[None, :] // GV)
    t = np.arange(CHUNK)[:, None]
    j = np.arange(GVW)[None, :] % CHUNK
    return (jnp.asarray(trif, BF16), jnp.asarray(trib, BF16), jnp.asarray(km, BF16), jnp.asarray(vm, BF16),
            jnp.asarray(j <= t, F32), jnp.asarray(j >= t, F32), jnp.asarray(km, F32))


def _gla(pg, pv, masks, s0f, s0b, plan, n_ctx):
    r = pg.shape[0]
    nsteps = plan[0].shape[0]
    fb = lambda u, f, b, *_: (f[u], 0)
    bb = lambda u, f, b, *_: (b[u], 0)
    const = lambda u, *_: (0, 0)
    sq = lambda u, f, b, s, *_: (s[u], 0, 0)
    oq = lambda u, f, b, s, o, *_: (o[u], 0, 0)
    grid_spec = pltpu.PrefetchScalarGridSpec(
        num_scalar_prefetch=6,
        grid=(nsteps,),
        in_specs=[
            pl.BlockSpec((BLK, PG_W), fb), pl.BlockSpec((BLK, GVW), fb),
            pl.BlockSpec((BLK, PG_W), bb), pl.BlockSpec((BLK, GVW), bb),
            pl.BlockSpec((BLK, BLK), const), pl.BlockSpec((BLK, BLK), const),
            pl.BlockSpec((GVW, GWP), const), pl.BlockSpec((GVW, GVW), const),
            pl.BlockSpec((CHUNK, GVW), const), pl.BlockSpec((CHUNK, GVW), const),
            pl.BlockSpec((GVW, GWP), const),
            pl.BlockSpec((None, GVW, GWP), sq), pl.BlockSpec((None, GVW, GWP), sq),
        ],
        out_specs=[
            pl.BlockSpec((BLK, GVW), fb), pl.BlockSpec((BLK, GVW), bb),
            pl.BlockSpec((None, GVW, GWP), oq), pl.BlockSpec((None, GVW, GWP), oq),
        ],
        scratch_shapes=[pltpu.VMEM((GVW, GWP), F32), pltpu.VMEM((GVW, GWP), F32)],
    )
    return pl.pallas_call(
        _gla_kernel,
        grid_spec=grid_spec,
        out_shape=[
            jax.ShapeDtypeStruct((r, GVW), BF16), jax.ShapeDtypeStruct((r, GVW), BF16),
            jax.ShapeDtypeStruct((n_ctx, GVW, GWP), F32), jax.ShapeDtypeStruct((n_ctx, GVW, GWP), F32),
        ],
        compiler_params=_cparams(("arbitrary",)),
    )(*plan, pg, pv, pg, pv, *masks, s0f, s0b)


def _pair_attention(q, keys, vals, biases):
    lane = lax.broadcasted_iota(jnp.int32, (1, 2 * HD), 1)
    first = lane < HD
    outs = []
    for half in range(2):
        qm = jnp.where(first if half == 0 else jnp.logical_not(first), q, jnp.zeros_like(q))
        ss = []
        for kk, bias in zip(keys, biases):
            s = _dot_nt(qm, kk)
            if bias is not None:
                s = s + bias[half]
            ss.append(s)
        m = ss[0].max(axis=-1, keepdims=True)
        for s in ss[1:]:
            m = jnp.maximum(m, s.max(axis=-1, keepdims=True))
        acc = None
        den = None
        for s, vv in zip(ss, vals):
            e = jnp.exp(s - m)
            d = e.sum(axis=-1, keepdims=True)
            o = _dot(e.astype(BF16), vv)
            acc = o if acc is None else acc + o
            den = d if den is None else den + d
        outs.append(acc / den)
    return jnp.where(first, outs[0], outs[1])


def _ctx_attn_kernel(q_ref, k_ref, v_ref, o_ref):
    for p in range(NH // 2):
        sl = slice(p * 2 * HD, (p + 1) * 2 * HD)
        o = _pair_attention(q_ref[:, sl], [k_ref[:, sl]], [v_ref[:, sl]], [None])
        o_ref[:, sl] = o.astype(o_ref.dtype)


def _ctx_attn(pn, n_ctx):
    return pl.pallas_call(
        _ctx_attn_kernel,
        grid=(n_ctx,),
        in_specs=[
            pl.BlockSpec((SEQ, NW), lambda b: (b, 0)),
            pl.BlockSpec((SEQ, NW), lambda b: (b, 1)),
            pl.BlockSpec((SEQ, NW), lambda b: (b, 2)),
        ],
        out_specs=pl.BlockSpec((SEQ, NW), lambda b: (b, 0)),
        out_shape=jax.ShapeDtypeStruct((n_ctx * SEQ, NW), BF16),
        compiler_params=_cparams(("arbitrary",)),
    )(pn, pn, pn)


QROWS = TQ // GRID_W
KPAIRS = BAND // (2 * GRID_W)
N_BIAS_BLK = 31


def _nat_block_table():
    rows = TLAT // GRID_W
    krows = BAND // GRID_W
    tbl = np.zeros((3, QROWS, KPAIRS), np.int32)
    for ty, (row0, ub) in enumerate(((0, 0), (QROWS, QROWS - krows // 4), (rows - QROWS, rows - krows))):
        for a in range(QROWS):
            qr = row0 + a
            bs = min(max(qr - NAT_KH // 2, 0), rows - NAT_KH)
            for kp in range(KPAIRS):
                kr0 = ub + 2 * kp
                v0 = bs <= kr0 < bs + NAT_KH
                v1 = bs <= kr0 + 1 < bs + NAT_KH
                a0 = kr0 - qr + NAT_KH - 1
                if v0 and v1:
                    tbl[ty, a, kp] = a0
                elif v1:
                    tbl[ty, a, kp] = 14 + a0 + 1
                elif v0:
                    tbl[ty, a, kp] = 22 + a0 - (NAT_KH - 1)
                else:
                    tbl[ty, a, kp] = N_BIAS_BLK - 1
    return tbl


_NAT_TBL = _nat_block_table()


def _nat_tile(win, idx_fn, q_ref, kb, vb, ck, cv, pb_ref, o_ref, s_scr, c_scr, e_scr, ec_scr):
    lane = lax.broadcasted_iota(jnp.int32, (1, 2 * HD), 1)
    first = lane < HD
    q = q_ref[...]
    w = 2 * GRID_W
    outs = []
    for half in range(2):
        qm = jnp.where(first if half == 0 else jnp.logical_not(first), q, jnp.zeros_like(q))
        s_scr[...] = _dot_nt(qm, kb)
        c_scr[...] = _dot_nt(qm, ck)
        dens = []
        for qr in range(QROWS):
            rows = slice(qr * GRID_W, (qr + 1) * GRID_W)
            lo, hi = win[qr]
            blocks = [s_scr[rows, kp * w:(kp + 1) * w] + pb_ref[half, idx_fn(qr, kp)]
                      for kp in range(lo, hi)]
            c0, c1 = c_scr[rows, 0:w], c_scr[rows, w:2 * w]
            mm = jnp.maximum(c0, c1)
            for b in blocks:
                mm = jnp.maximum(mm, b)
            m = mm.max(axis=-1, keepdims=True)
            e0, e1 = jnp.exp(c0 - m), jnp.exp(c1 - m)
            ec_scr[rows, 0:w] = e0.astype(BF16)
            ec_scr[rows, w:2 * w] = e1.astype(BF16)
            acc = e0 + e1
            for kp in range(KPAIRS):
                if lo <= kp < hi:
                    e = jnp.exp(blocks[kp - lo] - m)
                    acc = acc + e
                    e_scr[rows, kp * w:(kp + 1) * w] = e.astype(BF16)
                else:
                    e_scr[rows, kp * w:(kp + 1) * w] = jnp.zeros((GRID_W, w), BF16)
            dens.append(acc.sum(axis=-1, keepdims=True))
        o = _dot(e_scr[...], vb) + _dot(ec_scr[...], cv)
        outs.append(o / jnp.concatenate(dens, axis=0))
    o_ref[...] = jnp.where(first, outs[0], outs[1]).astype(o_ref.dtype)


def _nat_kernel(tbl, q_ref, k_ref, v_ref, ck_ref, cv_ref, pb_ref, o_ref, s_scr, c_scr, e_scr, ec_scr):
    j = pl.program_id(2)
    nj = pl.num_programs(2)
    start = pl.multiple_of(jnp.clip(j * TQ - BAND // 4, 0, TLAT - BAND), 256)
    kb = k_ref[pl.ds(start, BAND), :]
    vb = v_ref[pl.ds(start, BAND), :]
    args = (q_ref, kb, vb, ck_ref[...], cv_ref[...], pb_ref, o_ref, s_scr, c_scr, e_scr, ec_scr)
    edge = jnp.logical_or(j == 0, j == nj - 1)

    @pl.when(edge)
    def _():
        ty = jnp.where(j == 0, 0, 2)
        full = [(0, KPAIRS)] * QROWS
        _nat_tile(full, lambda qr, kp: tbl[(ty * QROWS + qr) * KPAIRS + kp], *args)

    @pl.when(jnp.logical_not(edge))
    def _():
        win = []
        for qr in range(QROWS):
            live = [kp for kp in range(KPAIRS) if _NAT_TBL[1, qr, kp] != N_BIAS_BLK - 1]
            win.append((live[0], live[-1] + 1))
        _nat_tile(win, lambda qr, kp: int(_NAT_TBL[1, qr, kp]), *args)


def _nat_attn(li, pn, ck, cv, pb, n_ctx_rows, n_lat):
    nj = TLAT // TQ
    qb0 = n_ctx_rows // TQ
    sb0 = n_ctx_rows // TLAT
    npair = NH // 2
    grid_spec = pltpu.PrefetchScalarGridSpec(
        num_scalar_prefetch=1,
        grid=(n_lat, npair, nj),
        in_specs=[
            pl.BlockSpec((TQ, 2 * HD), lambda b, p, j, t: (qb0 + b * nj + j, p)),
            pl.BlockSpec((TLAT, 2 * HD), lambda b, p, j, t: (sb0 + b, npair + p)),
            pl.BlockSpec((TLAT, 2 * HD), lambda b, p, j, t: (sb0 + b, 2 * npair + p)),
            pl.BlockSpec((None, None, SEQ, 2 * HD), lambda b, p, j, t: (li, b, 0, p)),
            pl.BlockSpec((None, None, SEQ, 2 * HD), lambda b, p, j, t: (li, b, 0, p)),
            pl.BlockSpec((None, 2, N_BIAS_BLK, GRID_W, 2 * GRID_W), lambda b, p, j, t: (li, p, 0, 0, 0)),
        ],
        out_specs=pl.BlockSpec((TQ, 2 * HD), lambda b, p, j, t: (b * nj + j, p)),
        scratch_shapes=[
            pltpu.VMEM((TQ, BAND), F32), pltpu.VMEM((TQ, SEQ), F32),
            pltpu.VMEM((TQ, BAND), BF16), pltpu.VMEM((TQ, SEQ), BF16),
        ],
    )
    return pl.pallas_call(
        _nat_kernel,
        grid_spec=grid_spec,
        out_shape=jax.ShapeDtypeStruct((n_lat * TLAT, NW), BF16),
        compiler_params=_cparams(("arbitrary", "arbitrary", "arbitrary")),
    )(jnp.asarray(_NAT_TBL.reshape(-1)), pn, pn, pn, ck, cv, pb)


def _proj_out_kernel(mrow, isctx, cidx, lidx, hprev, hnext, pblk, nblk,
                     x_ref, mod_ref, pc_ref, pcp_ref, pcn_ref, cw_ref, of_ref, ob_ref, gr_ref,
                     gng_ref, hm_ref, yc_ref, yl_ref, w_ref, o_ref):
    del mrow, cidx, lidx, pblk, nblk
    t = pl.program_id(0)
    pc = pc_ref[...].astype(F32)
    u = pc[:, CONV_W:2 * CONV_W] * pc[:, 2 * CONV_W:3 * CONV_W]
    pp = pcp_ref[7:8, :].astype(F32)
    pn = pcn_ref[0:1, :].astype(F32)
    u_prev_edge = pp[:, CONV_W:2 * CONV_W] * pp[:, 2 * CONV_W:3 * CONV_W] * hprev[t].astype(F32)
    u_next_edge = pn[:, CONV_W:2 * CONV_W] * pn[:, 2 * CONV_W:3 * CONV_W] * hnext[t].astype(F32)
    rows = lax.broadcasted_iota(jnp.int32, (TM_OUT, CONV_W), 0)
    u_prev = jnp.where(rows == 0, u_prev_edge, pltpu.roll(u, 1, 0))
    u_next = jnp.where(rows == TM_OUT - 1, u_next_edge, pltpu.roll(u, TM_OUT - 1, 0))
    ctx = isctx[t] == 1
    in_seq = rows & (SEQ - 1)
    u_prev = jnp.where(jnp.logical_and(ctx, in_seq == 0), 0.0, u_prev)
    u_next = jnp.where(jnp.logical_and(ctx, in_seq == SEQ - 1), 0.0, u_next)
    y_conv = pc[:, 0:CONV_W] * (cw_ref[0:1, :] * u_prev + cw_ref[1:2, :] * u + cw_ref[2:3, :] * u_next)

    o = of_ref[...].astype(F32) + ob_ref[...].astype(F32)
    on = o * lax.rsqrt(_head_mean(o * o, hm_ref[...]) + EPS) * gng_ref[...]
    y_gla = on * _silu(gr_ref[...].astype(F32))

    y_nat = jnp.where(isctx[t] == 1, yc_ref[...], yl_ref[...])

    y = (_dot(y_conv.astype(BF16), w_ref[0:CONV_W, :])
         + _dot(y_gla.astype(BF16), w_ref[CONV_W:CONV_W + GVW, :])
         + _dot(y_nat, w_ref[CONV_W + GVW:D, :]))
    o_ref[...] = x_ref[...] + mod_ref[2:3, :] * y


def _proj_out_plan(n_ctx, n_lat):
    nb = TLAT // TM_OUT
    nct = n_ctx * SEQ // TM_OUT
    nt = nct + n_lat * nb
    mrow, isctx, cidx, lidx, hprev, hnext, pblk, nblk = ([] for _ in range(8))
    per = TM_OUT // 8
    for t in range(nt):
        ctx = t < nct
        tl = t - nct
        mrow.append(0 if ctx else 1 + tl // nb)
        isctx.append(1 if ctx else 0)
        cidx.append(min(t, nct - 1))
        lidx.append(max(tl, 0))
        hprev.append(0 if ctx or tl % nb == 0 else 1)
        hnext.append(0 if ctx or tl % nb == nb - 1 else 1)
        pblk.append(max(t * per - 1, 0))
        nblk.append(min((t + 1) * per, nt * per - 1))
    return [np.asarray(a, np.int32) for a in (mrow, isctx, cidx, lidx, hprev, hnext, pblk, nblk)]


def _proj_out(li, x, mod, pc, conv_w, o_f, o_b, pv, gng, hm, y_ctx, y_lat, w_out, plan):
    r = x.shape[0]
    nt = plan[0].shape[0]
    row = lambda t, *_: (t, 0)
    const = lambda t, *_: (0, 0)
    layer = lambda t, *_: (li, 0, 0)
    grid_spec = pltpu.PrefetchScalarGridSpec(
        num_scalar_prefetch=8,
        grid=(nt,),
        in_specs=[
            pl.BlockSpec((TM_OUT, D), row),
            pl.BlockSpec((None, None, 6, D), lambda t, m, *_: (li, m[t], 0, 0)),
            pl.BlockSpec((TM_OUT, PC_W), row),
            pl.BlockSpec((8, PC_W), lambda t, m, ic, ci, lidx, hp, hn, pb, nb_: (pb[t], 0)),
            pl.BlockSpec((8, PC_W), lambda t, m, ic, ci, lidx, hp, hn, pb, nb_: (nb_[t], 0)),
            pl.BlockSpec((None, 3, CONV_W), layer),
            pl.BlockSpec((TM_OUT, GVW), row),
            pl.BlockSpec((TM_OUT, GVW), row),
            pl.BlockSpec((TM_OUT, GVW), lambda t, *_: (t, 1)),
            pl.BlockSpec((None, 1, GVW), layer),
            pl.BlockSpec((GVW, GVW), const),
            pl.BlockSpec((TM_OUT, NW), lambda t, m, ic, ci, *_: (ci[t], 0)),
            pl.BlockSpec((TM_OUT, NW), lambda t, m, ic, ci, lidx, *_: (lidx[t], 0)),
            pl.BlockSpec((None, D, D), layer),
        ],
        out_specs=pl.BlockSpec((TM_OUT, D), row),
    )
    return pl.pallas_call(
        _proj_out_kernel,
        grid_spec=grid_spec,
        out_shape=jax.ShapeDtypeStruct((r, D), F32),
        compiler_params=_cparams(("arbitrary",)),
    )(*plan, x, mod, pc, pc, pc, conv_w, o_f, o_b, pv, gng, hm, y_ctx, y_lat, w_out)


def _ffn_dense_kernel(mrow, x_ref, g_ref, mod_ref, wg_ref, wu_ref, wd_ref, o_ref, h_scr, acc):
    del mrow
    f = pl.program_id(1)

    @pl.when(f == 0)
    def _():
        h_scr[...] = _norm_mod(x_ref[...], g_ref[...], mod_ref[3:4, :], mod_ref[4:5, :]).astype(BF16)
        acc[...] = jnp.zeros_like(acc)

    h = h_scr[...]
    hid = _silu(_dot(h, wg_ref[...])) * _dot(h, wu_ref[...])
    acc[...] += _dot(hid.astype(BF16), wd_ref[...])

    @pl.when(f == pl.num_programs(1) - 1)
    def _():
        o_ref[...] = x_ref[...] + mod_ref[5:6, :] * acc[...]


def _ffn_dense(li, j, tm, tf, x, g2, mod, wg, wu, wd, mrow):
    r = x.shape[0]
    nf = D_FF // tf
    grid_spec = pltpu.PrefetchScalarGridSpec(
        num_scalar_prefetch=1,
        grid=(r // tm, nf),
        in_specs=[
            pl.BlockSpec((tm, D), lambda t, f, m: (t, 0)),
            pl.BlockSpec((None, 1, D), lambda t, f, m: (li, 0, 0)),
            pl.BlockSpec((None, None, 6, D), lambda t, f, m: (li, m[t], 0, 0)),
            pl.BlockSpec((None, D, tf), lambda t, f, m: (j, 0, f)),
            pl.BlockSpec((None, D, tf), lambda t, f, m: (j, 0, f)),
            pl.BlockSpec((None, tf, D), lambda t, f, m: (j, f, 0)),
        ],
        out_specs=pl.BlockSpec((tm, D), lambda t, f, m: (t, 0)),
        scratch_shapes=[pltpu.VMEM((tm, D), BF16), pltpu.VMEM((tm, D), F32)],
    )
    return pl.pallas_call(
        _ffn_dense_kernel,
        grid_spec=grid_spec,
        out_shape=jax.ShapeDtypeStruct((r, D), F32),
        compiler_params=_cparams(("arbitrary", "arbitrary")),
    )(mrow, x, g2, mod, wg, wu, wd)


def _router_kernel(mrow, x_ref, g_ref, mod_ref, rhi_ref, rlo_ref, h_ref, ri_ref, rw_ref):
    del mrow
    h = _norm_mod(x_ref[...], g_ref[...], mod_ref[3:4, :], mod_ref[4:5, :])
    h_ref[...] = h
    hhi, hlo = _split_bf16(h)
    logits = _dot(hhi, rhi_ref[...]) + _dot(hlo, rhi_ref[...]) + _dot(hhi, rlo_ref[...])
    lane = lax.broadcasted_iota(jnp.int32, logits.shape, 1).astype(F32)
    lg = jnp.where(lane < N_EXP, logits, -jnp.inf)
    m1 = lg.max(axis=-1, keepdims=True)
    i1 = jnp.where(lg == m1, lane, 128.0).min(axis=-1, keepdims=True)
    lg2 = jnp.where(lane == i1, -jnp.inf, lg)
    m2 = lg2.max(axis=-1, keepdims=True)
    i2 = jnp.where(lg2 == m2, lane, 128.0).min(axis=-1, keepdims=True)
    e = jnp.exp(m2 - m1)
    w1 = 1.0 / (1.0 + e)
    w2 = e / (1.0 + e)
    ri_ref[...] = jnp.where(lane == 0.0, i1, jnp.where(lane == 1.0, i2, 0.0)).astype(jnp.int32)
    rw_ref[...] = jnp.where(lane == 0.0, w1, jnp.where(lane == 1.0, w2, 0.0))


def _router(li, j, x, g2, mod, rhi, rlo, mrow):
    r = x.shape[0]
    row = lambda t, m: (t, 0)
    grid_spec = pltpu.PrefetchScalarGridSpec(
        num_scalar_prefetch=1,
        grid=(r // TM_PROJ,),
        in_specs=[
            pl.BlockSpec((TM_PROJ, D), row),
            pl.BlockSpec((None, 1, D), lambda t, m: (li, 0, 0)),
            pl.BlockSpec((None, None, 6, D), lambda t, m: (li, m[t], 0, 0)),
            pl.BlockSpec((None, D, 128), lambda t, m: (j, 0, 0)),
            pl.BlockSpec((None, D, 128), lambda t, m: (j, 0, 0)),
        ],
        out_specs=[
            pl.BlockSpec((TM_PROJ, D), row),
            pl.BlockSpec((TM_PROJ, 128), row),
            pl.BlockSpec((TM_PROJ, 128), row),
        ],
    )
    return pl.pallas_call(
        _router_kernel,
        grid_spec=grid_spec,
        out_shape=[
            jax.ShapeDtypeStruct((r, D), F32),
            jax.ShapeDtypeStruct((r, 128), jnp.int32),
            jax.ShapeDtypeStruct((r, 128), F32),
        ],
        compiler_params=_cparams(("arbitrary",)),
    )(mrow, x, g2, mod, rhi, rlo)


def _row_copy(src_ref, src_row, dst_ref, dst_row, sem):
    return pltpu.make_async_copy(src_ref.at[pl.ds(src_row, 1)], dst_ref.at[pl.ds(dst_row, 1)], sem)


def _dispatch_kernel(dest, pstart, plen, nused, h_ref, xg_hbm, zbuf, sem, zsem):
    t = pl.program_id(0)
    base = t * 2 * TD_DISP
    tm = zbuf.shape[0]
    n_tiles = xg_hbm.shape[0] // tm

    def start(rr, c):
        _row_copy(h_ref, rr, xg_hbm, dest[base + 2 * rr], sem).start()
        _row_copy(h_ref, rr, xg_hbm, dest[base + 2 * rr + 1], sem).start()
        return c

    lax.fori_loop(0, TD_DISP, start, 0, unroll=8)

    @pl.when(t == 0)
    def _():
        zbuf[...] = jnp.zeros_like(zbuf)
        for e in range(N_EXP):
            n = plen[e]
            s0 = pstart[e]
            head = n & 7
            pads = [(rr < head, _row_copy(zbuf, 0, xg_hbm, s0 + rr, zsem)) for rr in range(7)]
            for bit in reversed(range(3, tm.bit_length() - 1)):
                size = 1 << bit
                first = pl.multiple_of(s0 + head + ((n - head) & ~(2 * size - 1)), 8)
                pads.append(((n & size) != 0,
                             pltpu.make_async_copy(zbuf.at[pl.ds(0, size)], xg_hbm.at[pl.ds(first, size)], zsem)))
            for take, cp in pads:
                pl.when(take)(cp.start)
            for take, cp in pads:
                pl.when(take)(cp.wait)

        def tile_copy(i):
            return pltpu.make_async_copy(zbuf, xg_hbm.at[pl.ds(i * tm, tm)], zsem)

        def tstart(i, c):
            tile_copy(i).start()
            return c

        def twait(i, c):
            tile_copy(i).wait()
            return c

        lax.fori_loop(nused[0], n_tiles, tstart, 0)
        lax.fori_loop(nused[0], n_tiles, twait, 0)

    def wait(rr, c):
        _row_copy(h_ref, 0, xg_hbm, 0, sem).wait()
        return c

    lax.fori_loop(0, 2 * TD_DISP, wait, 0, unroll=8)


def _dispatch(tm, h, dest, pstart, plen, nused, n_tiles):
    r = h.shape[0]
    grid_spec = pltpu.PrefetchScalarGridSpec(
        num_scalar_prefetch=4,
        grid=(r // TD_DISP,),
        in_specs=[pl.BlockSpec((TD_DISP, D), lambda t, *_: (t, 0))],
        out_specs=pl.BlockSpec(memory_space=pl.ANY),
        scratch_shapes=[pltpu.VMEM((tm, D), F32), pltpu.SemaphoreType.DMA(()),
                        pltpu.SemaphoreType.DMA(())],
    )
    return pl.pallas_call(
        _dispatch_kernel,
        grid_spec=grid_spec,
        out_shape=jax.ShapeDtypeStruct((n_tiles * tm, D), F32),
        compiler_params=_cparams(("arbitrary",)),
    )(dest, pstart, plen, nused, h)


def _ffn_grouped_kernel(te, nused, x_ref, wg_ref, wu_ref, wd_ref, o_ref, xb, acc):
    del te
    i = pl.program_id(0)
    f = pl.program_id(1)
    last = pl.num_programs(1) - 1
    used = i < nused[0]

    @pl.when(jnp.logical_and(used, f == 0))
    def _():
        xb[...] = x_ref[...].astype(BF16)
        acc[...] = jnp.zeros_like(acc)

    @pl.when(used)
    def _():
        h = xb[...]
        hid = _silu(_dot(h, wg_ref[...].astype(BF16))) * _dot(h, wu_ref[...].astype(BF16))
        acc[...] += _dot(hid.astype(BF16), wd_ref[...].astype(BF16))

    @pl.when(jnp.logical_and(used, f == last))
    def _():
        o_ref[...] = acc[...]

    @pl.when(jnp.logical_and(jnp.logical_not(used), f == last))
    def _():
        o_ref[...] = jnp.zeros_like(o_ref)


def _ffn_grouped(j, tm, tf, xg, wg, wu, wd, te, nused, n_tiles):
    nf = D_EXP // tf
    fidx = lambda i, f, n: jnp.where(i < n[0], f, nf - 1)
    grid_spec = pltpu.PrefetchScalarGridSpec(
        num_scalar_prefetch=2,
        grid=(n_tiles, nf),
        in_specs=[
            pl.BlockSpec((tm, D), lambda i, f, e, n: (jnp.minimum(i, n[0] - 1), 0)),
            pl.BlockSpec((None, None, D, tf), lambda i, f, e, n: (j, e[i], 0, fidx(i, f, n))),
            pl.BlockSpec((None, None, D, tf), lambda i, f, e, n: (j, e[i], 0, fidx(i, f, n))),
            pl.BlockSpec((None, None, tf, D), lambda i, f, e, n: (j, e[i], fidx(i, f, n), 0)),
        ],
        out_specs=pl.BlockSpec((tm, D), lambda i, f, e, n: (i, 0)),
        scratch_shapes=[pltpu.VMEM((tm, D), BF16), pltpu.VMEM((tm, D), F32)],
    )
    return pl.pallas_call(
        _ffn_grouped_kernel,
        grid_spec=grid_spec,
        out_shape=jax.ShapeDtypeStruct((n_tiles * tm, D), F32),
        compiler_params=_cparams(("arbitrary", "arbitrary")),
    )(te, nused, xg, wg, wu, wd)


def _combine_kernel(mrow, dest, x_ref, mod_ref, rw_ref, y_hbm, o_ref, buf, sem):
    del mrow
    t = pl.program_id(0)
    base = t * TC_COMB * 2

    def start(rr, c):
        _row_copy(y_hbm, dest[base + rr], buf, rr, sem).start()
        return c

    lax.fori_loop(0, 2 * TC_COMB, start, 0, unroll=8)

    def wait(rr, c):
        _row_copy(y_hbm, 0, buf, 0, sem).wait()
        return c

    lax.fori_loop(0, 2 * TC_COMB, wait, 0, unroll=8)
    rw = rw_ref[...]
    y = rw[:, 0:1] * buf[0:TC_COMB, :] + rw[:, 1:2] * buf[TC_COMB:2 * TC_COMB, :]
    o_ref[...] = x_ref[...] + mod_ref[5:6, :] * y


def _combine(li, x, mod, rw, yg, dest, mrow):
    r = x.shape[0]
    grid_spec = pltpu.PrefetchScalarGridSpec(
        num_scalar_prefetch=2,
        grid=(r // TC_COMB,),
        in_specs=[
            pl.BlockSpec((TC_COMB, D), lambda t, m, d: (t, 0)),
            pl.BlockSpec((None, None, 6, D), lambda t, m, d: (li, m[t], 0, 0)),
            pl.BlockSpec((TC_COMB, 128), lambda t, m, d: (t, 0)),
            pl.BlockSpec(memory_space=pl.ANY),
        ],
        out_specs=pl.BlockSpec((TC_COMB, D), lambda t, m, d: (t, 0)),
        scratch_shapes=[pltpu.VMEM((2 * TC_COMB, D), F32), pltpu.SemaphoreType.DMA(())],
    )
    return pl.pallas_call(
        _combine_kernel,
        grid_spec=grid_spec,
        out_shape=jax.ShapeDtypeStruct((r, D), F32),
        compiler_params=_cparams(("arbitrary",)),
    )(mrow, dest, x, mod, rw, yg)


def _moe_plan(ri, tm, n_tiles):
    r = ri.shape[0]
    ef = ri[:, :2].reshape(-1)
    oh = (ef[:, None] == jnp.arange(N_EXP, dtype=jnp.int32)[None, :]).astype(jnp.int32)
    csum = jnp.cumsum(oh, axis=0)
    pos = jnp.sum(csum * oh, axis=1) - 1
    counts = csum[-1]
    tiles = (counts + tm - 1) // tm
    tile_end = jnp.cumsum(tiles)
    off = (tile_end - tiles) * tm
    dest = (jnp.sum(off[None, :] * oh, axis=1) + pos).astype(jnp.int32)
    tile_id = jnp.arange(n_tiles, dtype=jnp.int32)
    te = jnp.minimum(jnp.sum((tile_end[None, :] <= tile_id[:, None]).astype(jnp.int32), axis=1), N_EXP - 1)
    nused = tile_end[-1:].astype(jnp.int32)
    pstart = (off + counts).astype(jnp.int32)
    plen = (tiles * tm - counts).astype(jnp.int32)
    dest_t = dest.reshape(r // TC_COMB, TC_COMB, 2).transpose(0, 2, 1).reshape(-1)
    return dest, dest_t, te.astype(jnp.int32), nused, pstart, plen


def _mod_rows(n_ctx, n_lat, tile):
    rows = [0] * (n_ctx * SEQ // tile)
    for b in range(n_lat):
        rows += [1 + b] * (TLAT // tile)
    return np.asarray(rows, np.int32)


def _rope_tables():
    t = np.arange(TLAT)
    lane = np.arange(GWP)
    p = lane % GK
    sub = p % (GK // 2)
    nf = GK // 4
    freq = ROPE_BASE ** (-(sub % nf).astype(np.float32) / nf)
    pos = np.where((p < GK // 2)[None, :], (t // GRID_W)[:, None], (t % GRID_W)[:, None]).astype(np.float32)
    ang = jnp.asarray(pos) * jnp.asarray(freq.astype(np.float32))[None, :]
    cos, sin = jnp.cos(ang), jnp.sin(ang)
    lowhalf = jnp.asarray((sub < nf)[None, :])
    sina = jnp.where(lowhalf, -sin, 0.0)
    sinb = jnp.where(lowhalf, 0.0, sin)
    pad1 = jnp.ones((TM_PROJ, GWP), F32)
    pad0 = jnp.zeros((TM_PROJ, GWP), F32)
    return (jnp.concatenate([cos, pad1]), jnp.concatenate([sina, pad0]), jnp.concatenate([sinb, pad0]))


def _nat_bias_blocks(rpb):
    nl = rpb.shape[0]
    col = np.arange(GRID_W)
    c0 = np.clip(col - NAT_KW // 2, 0, GRID_W - NAT_KW)
    in_win = (col[None, :] >= c0[:, None]) & (col[None, :] < c0[:, None] + NAT_KW)
    dc = np.clip(col[None, :] - col[:, None], -(NAT_KW - 1), NAT_KW - 1) + NAT_KW - 1
    onehot = np.zeros((2 * NAT_KW - 1, GRID_W, GRID_W), np.float32)
    onehot[dc, col[:, None], col[None, :]] = 1.0
    sel = jnp.einsum("lhad,dqk->lhaqk", rpb, jnp.asarray(onehot), precision=lax.Precision.HIGHEST)
    cm = jnp.where(jnp.asarray(in_win)[None, None, None], sel, NEG)
    na = 2 * NAT_KH - 1
    neg = jnp.full((nl, NH, NAT_KH, GRID_W, GRID_W), NEG, F32)
    full = jnp.concatenate([cm[:, :, 0:na - 1], cm[:, :, 1:na]], axis=-1)
    left = jnp.concatenate([neg, cm[:, :, 0:NAT_KH]], axis=-1)
    right = jnp.concatenate([cm[:, :, NAT_KH - 1:na], neg], axis=-1)
    none = jnp.concatenate([neg[:, :, 0:1], neg[:, :, 0:1]], axis=-1)
    return jnp.concatenate([full, left, right, none], axis=2)


def _heads_to_rows(a):
    b, h, t, d = a.shape
    return a.transpose(0, 2, 1, 3).reshape(b, t, h * d)


def _rows_to_heads(a, b, t):
    return a.reshape(b, t, NH, HD).transpose(0, 2, 1, 3)


def _state_to_blockdiag(s):
    b = s.shape[0]
    eye = jnp.eye(GH, dtype=s.dtype)
    bd = jnp.einsum("bhkv,hg->bhvgk", s, eye).reshape(b, GVW, GW)
    return jnp.pad(bd, ((0, 0), (0, 0), (0, GWP - GW)))


def _blockdiag_to_state(st):
    b = st.shape[0]
    s5 = st[:, :, :GW].reshape(b, GH, GV, GH, GK)
    return jnp.stack([s5[:, h, :, h, :] for h in range(GH)], axis=1).transpose(0, 1, 3, 2)


def kernel(x_prompt, x_sample, cache_nat_k, cache_nat_v, state_gla, c, c_ctx, norm1_g, norm2_g, w_mod, b_mod, w_in, conv_w, gla_gate_w2, gla_gate_b, gla_norm_g, nat_q_norm_g, nat_k_norm_g, nat_rpb, w_out, ffn_w_gate, ffn_w_up, ffn_w_down, moe_router, moe_w_gate, moe_w_up, moe_w_down):
    n_ctx, n_lat = x_prompt.shape[0], x_sample.shape[0]
    depth = w_in.shape[0]
    ncr = n_ctx * SEQ
    r = ncr + n_lat * TLAT
    assert ncr % TLAT == 0 and n_lat + 1 <= 8

    x = jnp.concatenate([x_prompt.reshape(ncr, D), x_sample.reshape(n_lat * TLAT, D)], axis=0)
    cvecs = jnp.zeros((8, D), F32).at[0].set(c_ctx).at[1:1 + n_lat].set(c)
    mod = _modulation(cvecs, w_mod, b_mod)

    z = lambda n: jnp.zeros((depth, D, n), F32)
    w_in_p = jnp.concatenate([
        w_in[:, :, 768:960], z(GWP - GW), w_in[:, :, 960:1152], z(GWP - GW),
        w_in[:, :, 1920:1952], z(128 - 2 * LR),
        w_in[:, :, 0:768], w_in[:, :, 1152:1920], w_in[:, :, 1952:3104]], axis=-1).astype(BF16)
    w_out_b = w_out.astype(BF16)
    w2 = jnp.pad(gla_gate_w2, ((0, 0), (0, 0), (0, 0), (0, GWP - GW)))
    w2f = jnp.pad(w2[:, 0], ((0, 0), (0, 128 - LR), (0, 0))).astype(BF16)
    w2b = jnp.pad(w2[:, 1], ((0, 0), (LR, 128 - 2 * LR), (0, 0))).astype(BF16)
    gb = jnp.pad(gla_gate_b, ((0, 0), (0, 0), (0, GWP - GW)))[:, :, None, :]
    hm = jnp.asarray(np.kron(np.eye(NH), np.full((HD, HD), 1.0 / HD)), BF16)
    g1 = norm1_g[:, None, :]
    g2 = norm2_g[:, None, :]
    qg = jnp.tile(nat_q_norm_g, (1, NH))[:, None, :]
    kg = jnp.tile(nat_k_norm_g, (1, NH))[:, None, :]
    gng = jnp.tile(gla_norm_g, (1, GH))[:, None, :]
    gla_masks = _gla_masks()
    rope = _rope_tables()
    pb = _nat_bias_blocks(nat_rpb)
    ck = _heads_to_rows(cache_nat_k.transpose(1, 0, 2, 3, 4).reshape(depth * n_lat, NH, SEQ, HD))
    cv = _heads_to_rows(cache_nat_v.transpose(1, 0, 2, 3, 4).reshape(depth * n_lat, NH, SEQ, HD))
    ck = ck.reshape(depth, n_lat, SEQ, NW).astype(BF16)
    cv = cv.reshape(depth, n_lat, SEQ, NW).astype(BF16)
    ffn_g, ffn_u, ffn_d = ffn_w_gate.astype(BF16), ffn_w_up.astype(BF16), ffn_w_down.astype(BF16)
    router_p = jnp.pad(moe_router, ((0, 0), (0, 0), (0, 128 - N_EXP)))
    rhi = router_p.astype(BF16)
    rlo = (router_p - rhi.astype(F32)).astype(BF16)

    mrow_proj = jnp.asarray(_mod_rows(n_ctx, n_lat, TM_PROJ))
    in_plan = [jnp.asarray(a) for a in _proj_in_plan(n_ctx, n_lat)]
    mrow_comb = jnp.asarray(_mod_rows(n_ctx, n_lat, TC_COMB))
    gla_plan = [jnp.asarray(a) for a in _gla_plan(n_ctx, n_lat)]
    out_plan = [jnp.asarray(a) for a in _proj_out_plan(n_ctx, n_lat)]

    new_k, new_v, new_s = [], [], []
    for i in range(depth):
        pg, pc, pv, pn, knf, vnf = _proj_in(i, x, g1, mod, w_in_p, hm, qg, kg, rope, w2f, w2b, gb,
                                            in_plan, ncr)
        s0f = _state_to_blockdiag(state_gla[:, i, 0])
        s0b = _state_to_blockdiag(state_gla[:, i, 1])
        o_f, o_b, sff, sfb = _gla(pg, pv, gla_masks, s0f, s0b, gla_plan, n_ctx)
        y_ctx = _ctx_attn(pn, n_ctx)
        y_lat = _nat_attn(i, pn, ck, cv, pb, ncr, n_lat)
        x = _proj_out(i, x, mod, pc, conv_w, o_f, o_b, pv, gng, hm, y_ctx, y_lat, w_out_b, out_plan)
        j = i // 2
        if i % 2 == 0:
            tm, tf = FFN_TILE
            mrow_ffn = jnp.asarray(_mod_rows(n_ctx, n_lat, tm))
            x = _ffn_dense(i, j, tm, tf, x, g2, mod, ffn_g, ffn_u, ffn_d, mrow_ffn)
        else:
            tm, tf = MOE_TILE
            n_tiles = 2 * r // tm + N_EXP
            h, ri, rw = _router(i, j, x, g2, mod, rhi, rlo, mrow_proj)
            dest, dest_t, te, nused, pstart, plen = _moe_plan(ri, tm, n_tiles)
            xg = _dispatch(tm, h, dest, pstart, plen, nused, n_tiles)
            yg = _ffn_grouped(j, tm, tf, xg, moe_w_gate, moe_w_up, moe_w_down, te, nused, n_tiles)
            x = _combine(i, x, mod, rw, yg, dest_t, mrow_comb)
        new_k.append(_rows_to_heads(knf, n_ctx, SEQ))
        new_v.append(_rows_to_heads(vnf, n_ctx, SEQ))
        new_s.append(jnp.stack([_blockdiag_to_state(sff), _blockdiag_to_state(sfb)], axis=1))

    y_prompt = x[:ncr].reshape(n_ctx, SEQ, D)
    y_sample = x[ncr:].reshape(n_lat, TLAT, D)
    return (y_prompt, y_sample, jnp.stack(new_k, axis=1), jnp.stack(new_v, axis=1), jnp.stack(new_s, axis=1))
```

```python
import numpy as np
import jax
import jax.numpy as jnp
from jax import lax
from jax.experimental import pallas as pl
from jax.experimental.pallas import tpu as pltpu

F32 = jnp.float32
BF16 = jnp.bfloat16

D = 1024
SEQ = 256
TLAT = 4096
GRID_W = 64
HD = 64
CONV_W = 256
GH, GK, GV = 6, 32, 64
NH = 6
LR = 16
GATE_NORM = 16.0
CHUNK = 64
NAT_KH, NAT_KW = 8, 16
ROPE_BASE = 10000.0
D_FF = 2816
N_EXP = 8
D_EXP = 3584
EPS = 1e-6
NEG = -1e30

GW = GH * GK
GWP = 256
GVW = GH * GV
NW = NH * HD

PG_IN = 640
PG_W = 1024
PC_W = 768
PV_W = 768
PN_W = 1152
W_IN_P = PG_IN + PC_W + PV_W + PN_W

TM_PROJ = 512
BLK = 256
TM_OUT = 512
TQ = 512
BAND = 1024
FFN_TILE = (512, 1408)
MOE_TILE = (1024, 512)
TC_COMB = 256
TD_DISP = 256

VMEM_LIMIT = 56 * 1024 * 1024


def _cparams(sem):
    return pltpu.CompilerParams(dimension_semantics=sem, vmem_limit_bytes=VMEM_LIMIT)


def _dot(a, b):
    return jnp.dot(a, b, preferred_element_type=F32)


def _dot_nt(a, b):
    return lax.dot_general(a, b, (((1,), (1,)), ((), ())), preferred_element_type=F32)


def _dot_tn(a, b):
    return lax.dot_general(a, b, (((0,), (0,)), ((), ())), preferred_element_type=F32)


def _split_bf16(a):
    hi = a.astype(BF16)
    lo = (a - hi.astype(F32)).astype(BF16)
    return hi, lo


def _silu(a):
    return a * jax.nn.sigmoid(a)


def _head_mean(sq, hm):
    hi, lo = _split_bf16(sq)
    return _dot(hi, hm) + _dot(lo, hm)


def _mod_kernel(c_ref, w_ref, b_ref, o_ref):
    s = _silu(c_ref[...])
    o_ref[...] = _dot(s.astype(BF16), w_ref[...].astype(BF16)) + b_ref[...]


def _modulation(cvecs, w_mod, b_mod):
    nl = w_mod.shape[0]
    out = pl.pallas_call(
        _mod_kernel,
        grid=(nl, 6),
        in_specs=[
            pl.BlockSpec((8, D), lambda l, j: (0, 0)),
            pl.BlockSpec((None, D, D), lambda l, j: (l, 0, j)),
            pl.BlockSpec((None, 1, D), lambda l, j: (l, 0, j)),
        ],
        out_specs=pl.BlockSpec((None, 8, D), lambda l, j: (l, 0, j)),
        out_shape=jax.ShapeDtypeStruct((nl, 8, 6 * D), F32),
        compiler_params=_cparams(("arbitrary", "arbitrary")),
    )(cvecs, w_mod, b_mod.reshape(nl, 1, 6 * D))
    return out.reshape(nl, 8, 6, D)


def _norm_mod(x, g, shift, scale):
    ms = jnp.mean(x * x, axis=-1, keepdims=True)
    return x * lax.rsqrt(ms + EPS) * g * (1.0 + scale) + shift


def _log_sigmoid(z):
    return jnp.minimum(z, 0.0) - jnp.log1p(jnp.exp(-jnp.abs(z)))


def _rope(a, cos, sina, sinb):
    return a * cos + pltpu.roll(a, GWP - 8, 1) * sina + pltpu.roll(a, 8, 1) * sinb


def _proj_in_kernel(rtile, mrow, tbl, kvblk, x_ref, g_ref, mod_ref, w_ref, hm_ref, qg_ref, kg_ref,
                    cos_ref, sina_ref, sinb_ref, w2f_ref, w2b_ref, gbf_ref, gbb_ref,
                    pg_ref, pc_ref, pv_ref, pn_ref, knf_ref, vnf_ref):
    del rtile, mrow, tbl, kvblk
    h = _norm_mod(x_ref[...], g_ref[...], mod_ref[0:1, :], mod_ref[1:2, :]).astype(BF16)
    gq = _dot(h, w_ref[:, 0:GWP]) * (GK ** -0.5)
    gk = _dot(h, w_ref[:, GWP:2 * GWP])
    lr = _dot(h, w_ref[:, 2 * GWP:PG_IN]).astype(BF16)
    cos, sina, sinb = cos_ref[...], sina_ref[...], sinb_ref[...]
    pg_ref[:, 0:GWP] = _rope(gq, cos, sina, sinb).astype(BF16)
    pg_ref[:, GWP:2 * GWP] = _rope(gk, cos, sina, sinb).astype(BF16)
    gf = _log_sigmoid(_dot(lr, w2f_ref[...]) + gbf_ref[...]) * (1.0 / GATE_NORM)
    gb = _log_sigmoid(_dot(lr, w2b_ref[...]) + gbb_ref[...]) * (1.0 / GATE_NORM)
    pg_ref[:, 2 * GWP:3 * GWP] = gf.astype(BF16)
    pg_ref[:, 3 * GWP:4 * GWP] = gb.astype(BF16)
    pc_ref[...] = _dot(h, w_ref[:, PG_IN:PG_IN + PC_W]).astype(BF16)
    pv_ref[...] = _dot(h, w_ref[:, PG_IN + PC_W:PG_IN + PC_W + PV_W]).astype(BF16)
    o = PG_IN + PC_W + PV_W
    nq = _dot(h, w_ref[:, o:o + NW])
    nk = _dot(h, w_ref[:, o + NW:o + 2 * NW])
    nv = _dot(h, w_ref[:, o + 2 * NW:o + 3 * NW])
    hm = hm_ref[...]
    qn = nq * lax.rsqrt(_head_mean(nq * nq, hm) + EPS) * qg_ref[...]
    kn = nk * lax.rsqrt(_head_mean(nk * nk, hm) + EPS) * kg_ref[...]
    pn_ref[:, 0:NW] = (qn * (HD ** -0.5)).astype(BF16)
    pn_ref[:, NW:2 * NW] = kn.astype(BF16)
    pn_ref[:, 2 * NW:3 * NW] = nv.astype(BF16)
    knf_ref[...] = kn
    vnf_ref[...] = nv


def _proj_in_plan(n_ctx, n_lat):
    nct = n_ctx * SEQ // TM_PROJ
    per = TLAT // TM_PROJ
    rtile, mrow, tbl, kvblk = [], [], [], []
    for b in range(n_lat):
        for j in range(per):
            rtile.append(nct + b * per + j); mrow.append(1 + b); tbl.append(j); kvblk.append(0)
    for t in range(nct):
        rtile.append(t); mrow.append(0); tbl.append(per); kvblk.append(t)
    return [np.asarray(a, np.int32) for a in (rtile, mrow, tbl, kvblk)]


def _proj_in(li, x, g1, mod, w_in_p, hm, qg, kg, rope, w2f, w2b, gb, plan, n_ctx_rows):
    r = x.shape[0]
    nt = r // TM_PROJ
    cos, sina, sinb = rope
    row = lambda t, rt, *_: (rt[t], 0)
    const = lambda t, *_: (0, 0)
    layer = lambda t, *_: (li, 0, 0)
    tab = pl.BlockSpec((TM_PROJ, GWP), lambda t, rt, m, tb, kv: (tb[t], 0))
    kvo = pl.BlockSpec((TM_PROJ, NW), lambda t, rt, m, tb, kv: (kv[t], 0))
    grid_spec = pltpu.PrefetchScalarGridSpec(
        num_scalar_prefetch=4,
        grid=(nt,),
        in_specs=[
            pl.BlockSpec((TM_PROJ, D), row),
            pl.BlockSpec((None, 1, D), layer),
            pl.BlockSpec((None, None, 6, D), lambda t, rt, m, *_: (li, m[t], 0, 0)),
            pl.BlockSpec((None, D, W_IN_P), layer),
            pl.BlockSpec((NW, NW), const),
            pl.BlockSpec((None, 1, NW), layer),
            pl.BlockSpec((None, 1, NW), layer),
            tab, tab, tab,
            pl.BlockSpec((None, 128, GWP), layer), pl.BlockSpec((None, 128, GWP), layer),
            pl.BlockSpec((None, None, 1, GWP), lambda t, *_: (li, 0, 0, 0)),
            pl.BlockSpec((None, None, 1, GWP), lambda t, *_: (li, 1, 0, 0)),
        ],
        out_specs=[
            pl.BlockSpec((TM_PROJ, PG_W), row),
            pl.BlockSpec((TM_PROJ, PC_W), row),
            pl.BlockSpec((TM_PROJ, PV_W), row),
            pl.BlockSpec((TM_PROJ, PN_W), row),
            kvo, kvo,
        ],
    )
    return pl.pallas_call(
        _proj_in_kernel,
        grid_spec=grid_spec,
        out_shape=[
            jax.ShapeDtypeStruct((r, PG_W), BF16),
            jax.ShapeDtypeStruct((r, PC_W), BF16),
            jax.ShapeDtypeStruct((r, PV_W), BF16),
            jax.ShapeDtypeStruct((r, PN_W), BF16),
            jax.ShapeDtypeStruct((n_ctx_rows, NW), F32),
            jax.ShapeDtypeStruct((n_ctx_rows, NW), F32),
        ],
        compiler_params=_cparams(("arbitrary",)),
    )(*plan, x, g1, mod, w_in_p, hm, qg, kg, cos, sina, sinb, w2f, w2b, gb, gb)


def _gla_direction(pg, v, tri, km, vm, am, sm, st_ref, o_ref, rev):
    q = pg[:, 0:GWP].astype(F32)
    k = pg[:, GWP:2 * GWP].astype(F32)
    g = pg[:, 3 * GWP:4 * GWP] if rev else pg[:, 2 * GWP:3 * GWP]
    cum = _dot(tri, g)
    qd = (q * jnp.exp(cum)).astype(BF16)
    kd = (k * jnp.exp(-cum)).astype(BF16)
    chunks = range(BLK // CHUNK)
    for c in (reversed(chunks) if rev else chunks):
        lo = c * CHUNK
        sl = slice(lo, lo + CHUNK)
        edge = lo if rev else lo + CHUNK - 1
        cend = cum[edge:edge + 1, :]
        kst = (k[sl] * jnp.exp(cend - cum[sl])).astype(BF16)
        decay = jnp.exp(cend)
        kblk = jnp.concatenate([kd[sl]] * GH, axis=0) * km
        a = jnp.where(am > 0.0, _dot_nt(qd[sl], kblk), 0.0).astype(BF16)
        v_c = v[sl]
        vblk = jnp.concatenate([v_c] * GH, axis=0) * vm
        st = st_ref[...]
        o = _dot(a, vblk) + _dot_nt(qd[sl], st.astype(BF16))
        o_ref[sl, :] = o.astype(o_ref.dtype)
        ut = _dot_tn(v_c, kst)
        st_ref[...] = st * decay + ut * sm


def _gla_kernel(fblk, bblk, sidx, oidx, first, isctx,
                pgf_ref, pvf_ref, pgb_ref, pvb_ref, trif_ref, trib_ref, km_ref, vm_ref,
                amf_ref, amb_ref, sm_ref, s0f_ref, s0b_ref,
                of_ref, ob_ref, sff_ref, sfb_ref, stf, stb):
    del fblk, bblk, sidx, oidx
    u = pl.program_id(0)

    @pl.when(first[u] == 1)
    def _():
        ctx = isctx[u] == 1
        stf[...] = jnp.where(ctx, 0.0, s0f_ref[...])
        stb[...] = jnp.where(ctx, 0.0, s0b_ref[...])

    km, vm, sm = km_ref[...], vm_ref[...], sm_ref[...]
    _gla_direction(pgf_ref[...], pvf_ref[...], trif_ref[...], km, vm, amf_ref[...], sm, stf, of_ref, False)
    _gla_direction(pgb_ref[...], pvb_ref[...], trib_ref[...], km, vm, amb_ref[...], sm, stb, ob_ref, True)
    sff_ref[...] = stf[...]
    sfb_ref[...] = stb[...]


def _gla_plan(n_ctx, n_lat):
    nb = TLAT // BLK
    fblk, bblk, sidx, oidx, first, isctx = [], [], [], [], [], []
    for s in range(n_lat):
        for j in range(nb):
            fblk.append(n_ctx + s * nb + j)
            bblk.append(n_ctx + s * nb + nb - 1 - j)
            sidx.append(s); oidx.append(0); first.append(1 if j == 0 else 0); isctx.append(0)
    for s in range(n_ctx):
        fblk.append(s); bblk.append(s); sidx.append(0); oidx.append(s); first.append(1); isctx.append(1)
    return [np.asarray(a, np.int32) for a in (fblk, bblk, sidx, oidx, first, isctx)]


def _gla_masks():
    ii = np.arange(BLK)
    same = (ii[:, None] // CHUNK) == (ii[None, :] // CHUNK)
    trif = same & (ii[None, :] <= ii[:, None])
    trib = same & (ii[None, :] >= ii[:, None])
    rk = np.arange(GVW)[:, None] // CHUNK
    km = rk == (np.arange(GWP)[None, :] // GK)
    vm = rk == (np.arange(GVW)[None, :] // GV)
    t = np.arange(CHUNK)[:, None]
    j = np.arange(GVW)[None, :] % CHUNK
    return (jnp.asarray(trif, BF16), jnp.asarray(trib, BF16), jnp.asarray(km, BF16), jnp.asarray(vm, BF16),
            jnp.asarray(j <= t, F32), jnp.asarray(j >= t, F32), jnp.asarray(km, F32))


def _gla(pg, pv, masks, s0f, s0b, plan, n_ctx):
    r = pg.shape[0]
    nsteps = plan[0].shape[0]
    fb = lambda u, f, b, *_: (f[u], 0)
    bb = lambda u, f, b, *_: (b[u], 0)
    const = lambda u, *_: (0, 0)
    sq = lambda u, f, b, s, *_: (s[u], 0, 0)
    oq = lambda u, f, b, s, o, *_: (o[u], 0, 0)
    grid_spec = pltpu.PrefetchScalarGridSpec(
        num_scalar_prefetch=6,
        grid=(nsteps,),
        in_specs=[
            pl.BlockSpec((BLK, PG_W), fb), pl.BlockSpec((BLK, GVW), fb),
            pl.BlockSpec((BLK, PG_W), bb), pl.BlockSpec((BLK, GVW), bb),
            pl.BlockSpec((BLK, BLK), const), pl.BlockSpec((BLK, BLK), const),
            pl.BlockSpec((GVW, GWP), const), pl.BlockSpec((GVW, GVW), const),
            pl.BlockSpec((CHUNK, GVW), const), pl.BlockSpec((CHUNK, GVW), const),
            pl.BlockSpec((GVW, GWP), const),
            pl.BlockSpec((None, GVW, GWP), sq), pl.BlockSpec((None, GVW, GWP), sq),
        ],
        out_specs=[
            pl.BlockSpec((BLK, GVW), fb), pl.BlockSpec((BLK, GVW), bb),
            pl.BlockSpec((None, GVW, GWP), oq), pl.BlockSpec((None, GVW, GWP), oq),
        ],
        scratch_shapes=[pltpu.VMEM((GVW, GWP), F32), pltpu.VMEM((GVW, GWP), F32)],
    )
    return pl.pallas_call(
        _gla_kernel,
        grid_spec=grid_spec,
        out_shape=[
            jax.ShapeDtypeStruct((r, GVW), BF16), jax.ShapeDtypeStruct((r, GVW), BF16),
            jax.ShapeDtypeStruct((n_ctx, GVW, GWP), F32), jax.ShapeDtypeStruct((n_ctx, GVW, GWP), F32),
        ],
        compiler_params=_cparams(("arbitrary",)),
    )(*plan, pg, pv, pg, pv, *masks, s0f, s0b)


def _pair_attention(q, keys, vals, biases):
    lane = lax.broadcasted_iota(jnp.int32, (1, 2 * HD), 1)
    first = lane < HD
    outs = []
    for half in range(2):
        qm = jnp.where(first if half == 0 else jnp.logical_not(first), q, jnp.zeros_like(q))
        ss = []
        for kk, bias in zip(keys, biases):
            s = _dot_nt(qm, kk)
            if bias is not None:
                s = s + bias[half]
            ss.append(s)
        m = ss[0].max(axis=-1, keepdims=True)
        for s in ss[1:]:
            m = jnp.maximum(m, s.max(axis=-1, keepdims=True))
        acc = None
        den = None
        for s, vv in zip(ss, vals):
            e = jnp.exp(s - m)
            d = e.sum(axis=-1, keepdims=True)
            o = _dot(e.astype(BF16), vv)
            acc = o if acc is None else acc + o
            den = d if den is None else den + d
        outs.append(acc / den)
    return jnp.where(first, outs[0], outs[1])


def _ctx_attn_kernel(q_ref, k_ref, v_ref, o_ref):
    for p in range(NH // 2):
        sl = slice(p * 2 * HD, (p + 1) * 2 * HD)
        o = _pair_attention(q_ref[:, sl], [k_ref[:, sl]], [v_ref[:, sl]], [None])
        o_ref[:, sl] = o.astype(o_ref.dtype)


def _ctx_attn(pn, n_ctx):
    return pl.pallas_call(
        _ctx_attn_kernel,
        grid=(n_ctx,),
        in_specs=[
            pl.BlockSpec((SEQ, NW), lambda b: (b, 0)),
            pl.BlockSpec((SEQ, NW), lambda b: (b, 1)),
            pl.BlockSpec((SEQ, NW), lambda b: (b, 2)),
        ],
        out_specs=pl.BlockSpec((SEQ, NW), lambda b: (b, 0)),
        out_shape=jax.ShapeDtypeStruct((n_ctx * SEQ, NW), BF16),
        compiler_params=_cparams(("arbitrary",)),
    )(pn, pn, pn)


QROWS = TQ // GRID_W
KPAIRS = BAND // (2 * GRID_W)
N_BIAS_BLK = 31


def _nat_block_table():
    rows = TLAT // GRID_W
    krows = BAND // GRID_W
    tbl = np.zeros((3, QROWS, KPAIRS), np.int32)
    for ty, (row0, ub) in enumerate(((0, 0), (QROWS, QROWS - krows // 4), (rows - QROWS, rows - krows))):
        for a in range(QROWS):
            qr = row0 + a
            bs = min(max(qr - NAT_KH // 2, 0), rows - NAT_KH)
            for kp in range(KPAIRS):
                kr0 = ub + 2 * kp
                v0 = bs <= kr0 < bs + NAT_KH
                v1 = bs <= kr0 + 1 < bs + NAT_KH
                a0 = kr0 - qr + NAT_KH - 1
                if v0 and v1:
                    tbl[ty, a, kp] = a0
                elif v1:
                    tbl[ty, a, kp] = 14 + a0 + 1
                elif v0:
                    tbl[ty, a, kp] = 22 + a0 - (NAT_KH - 1)
                else:
                    tbl[ty, a, kp] = N_BIAS_BLK - 1
    return tbl


_NAT_TBL = _nat_block_table()


def _nat_tile(win, idx_fn, q_ref, kb, vb, ck, cv, pb_ref, o_ref, s_scr, c_scr, e_scr, ec_scr):
    lane = lax.broadcasted_iota(jnp.int32, (1, 2 * HD), 1)
    first = lane < HD
    q = q_ref[...]
    w = 2 * GRID_W
    outs = []
    for half in range(2):
        qm = jnp.where(first if half == 0 else jnp.logical_not(first), q, jnp.zeros_like(q))
        s_scr[...] = _dot_nt(qm, kb)
        c_scr[...] = _dot_nt(qm, ck)
        dens = []
        for qr in range(QROWS):
            rows = slice(qr * GRID_W, (qr + 1) * GRID_W)
            lo, hi = win[qr]
            blocks = [s_scr[rows, kp * w:(kp + 1) * w] + pb_ref[half, idx_fn(qr, kp)]
                      for kp in range(lo, hi)]
            c0, c1 = c_scr[rows, 0:w], c_scr[rows, w:2 * w]
            mm = jnp.maximum(c0, c1)
            for b in blocks:
                mm = jnp.maximum(mm, b)
            m = mm.max(axis=-1, keepdims=True)
            e0, e1 = jnp.exp(c0 - m), jnp.exp(c1 - m)
            ec_scr[rows, 0:w] = e0.astype(BF16)
            ec_scr[rows, w:2 * w] = e1.astype(BF16)
            acc = e0 + e1
            for kp in range(KPAIRS):
                if lo <= kp < hi:
                    e = jnp.exp(blocks[kp - lo] - m)
                    acc = acc + e
                    e_scr[rows, kp * w:(kp + 1) * w] = e.astype(BF16)
                else:
                    e_scr[rows, kp * w:(kp + 1) * w] = jnp.zeros((GRID_W, w), BF16)
            dens.append(acc.sum(axis=-1, keepdims=True))
        o = _dot(e_scr[...], vb) + _dot(ec_scr[...], cv)
        outs.append(o / jnp.concatenate(dens, axis=0))
    o_ref[...] = jnp.where(first, outs[0], outs[1]).astype(o_ref.dtype)


def _nat_kernel(tbl, q_ref, k_ref, v_ref, ck_ref, cv_ref, pb_ref, o_ref, s_scr, c_scr, e_scr, ec_scr):
    j = pl.program_id(2)
    nj = pl.num_programs(2)
    start = pl.multiple_of(jnp.clip(j * TQ - BAND // 4, 0, TLAT - BAND), 256)
    kb = k_ref[pl.ds(start, BAND), :]
    vb = v_ref[pl.ds(start, BAND), :]
    args = (q_ref, kb, vb, ck_ref[...], cv_ref[...], pb_ref, o_ref, s_scr, c_scr, e_scr, ec_scr)
    edge = jnp.logical_or(j == 0, j == nj - 1)

    @pl.when(edge)
    def _():
        ty = jnp.where(j == 0, 0, 2)
        full = [(0, KPAIRS)] * QROWS
        _nat_tile(full, lambda qr, kp: tbl[(ty * QROWS + qr) * KPAIRS + kp], *args)

    @pl.when(jnp.logical_not(edge))
    def _():
        win = []
        for qr in range(QROWS):
            live = [kp for kp in range(KPAIRS) if _NAT_TBL[1, qr, kp] != N_BIAS_BLK - 1]
            win.append((live[0], live[-1] + 1))
        _nat_tile(win, lambda qr, kp: int(_NAT_TBL[1, qr, kp]), *args)


def _nat_attn(li, pn, ck, cv, pb, n_ctx_rows, n_lat):
    nj = TLAT // TQ
    qb0 = n_ctx_rows // TQ
    sb0 = n_ctx_rows // TLAT
    npair = NH // 2
    grid_spec = pltpu.PrefetchScalarGridSpec(
        num_scalar_prefetch=1,
        grid=(n_lat, npair, nj),
        in_specs=[
            pl.BlockSpec((TQ, 2 * HD), lambda b, p, j, t: (qb0 + b * nj + j, p)),
            pl.BlockSpec((TLAT, 2 * HD), lambda b, p, j, t: (sb0 + b, npair + p)),
            pl.BlockSpec((TLAT, 2 * HD), lambda b, p, j, t: (sb0 + b, 2 * npair + p)),
            pl.BlockSpec((None, None, SEQ, 2 * HD), lambda b, p, j, t: (li, b, 0, p)),
            pl.BlockSpec((None, None, SEQ, 2 * HD), lambda b, p, j, t: (li, b, 0, p)),
            pl.BlockSpec((None, 2, N_BIAS_BLK, GRID_W, 2 * GRID_W), lambda b, p, j, t: (li, p, 0, 0, 0)),
        ],
        out_specs=pl.BlockSpec((TQ, 2 * HD), lambda b, p, j, t: (b * nj + j, p)),
        scratch_shapes=[
            pltpu.VMEM((TQ, BAND), F32), pltpu.VMEM((TQ, SEQ), F32),
            pltpu.VMEM((TQ, BAND), BF16), pltpu.VMEM((TQ, SEQ), BF16),
        ],
    )
    return pl.pallas_call(
        _nat_kernel,
        grid_spec=grid_spec,
        out_shape=jax.ShapeDtypeStruct((n_lat * TLAT, NW), BF16),
        compiler_params=_cparams(("arbitrary", "arbitrary", "arbitrary")),
    )(jnp.asarray(_NAT_TBL.reshape(-1)), pn, pn, pn, ck, cv, pb)


def _proj_out_kernel(mrow, isctx, cidx, lidx, hprev, hnext, pblk, nblk,
                     x_ref, mod_ref, pc_ref, pcp_ref, pcn_ref, cw_ref, of_ref, ob_ref, gr_ref,
                     gng_ref, hm_ref, yc_ref, yl_ref, w_ref, o_ref):
    del mrow, cidx, lidx, pblk, nblk
    t = pl.program_id(0)
    pc = pc_ref[...].astype(F32)
    u = pc[:, CONV_W:2 * CONV_W] * pc[:, 2 * CONV_W:3 * CONV_W]
    pp = pcp_ref[7:8, :].astype(F32)
    pn = pcn_ref[0:1, :].astype(F32)
    u_prev_edge = pp[:, CONV_W:2 * CONV_W] * pp[:, 2 * CONV_W:3 * CONV_W] * hprev[t].astype(F32)
    u_next_edge = pn[:, CONV_W:2 * CONV_W] * pn[:, 2 * CONV_W:3 * CONV_W] * hnext[t].astype(F32)
    rows = lax.broadcasted_iota(jnp.int32, (TM_OUT, CONV_W), 0)
    u_prev = jnp.where(rows == 0, u_prev_edge, pltpu.roll(u, 1, 0))
    u_next = jnp.where(rows == TM_OUT - 1, u_next_edge, pltpu.roll(u, TM_OUT - 1, 0))
    ctx = isctx[t] == 1
    in_seq = rows & (SEQ - 1)
    u_prev = jnp.where(jnp.logical_and(ctx, in_seq == 0), 0.0, u_prev)
    u_next = jnp.where(jnp.logical_and(ctx, in_seq == SEQ - 1), 0.0, u_next)
    y_conv = pc[:, 0:CONV_W] * (cw_ref[0:1, :] * u_prev + cw_ref[1:2, :] * u + cw_ref[2:3, :] * u_next)

    o = of_ref[...].astype(F32) + ob_ref[...].astype(F32)
    on = o * lax.rsqrt(_head_mean(o * o, hm_ref[...]) + EPS) * gng_ref[...]
    y_gla = on * _silu(gr_ref[...].astype(F32))

    y_nat = jnp.where(isctx[t] == 1, yc_ref[...], yl_ref[...])

    y = (_dot(y_conv.astype(BF16), w_ref[0:CONV_W, :])
         + _dot(y_gla.astype(BF16), w_ref[CONV_W:CONV_W + GVW, :])
         + _dot(y_nat, w_ref[CONV_W + GVW:D, :]))
    o_ref[...] = x_ref[...] + mod_ref[2:3, :] * y


def _proj_out_plan(n_ctx, n_lat):
    nb = TLAT // TM_OUT
    nct = n_ctx * SEQ // TM_OUT
    nt = nct + n_lat * nb
    mrow, isctx, cidx, lidx, hprev, hnext, pblk, nblk = ([] for _ in range(8))
    per = TM_OUT // 8
    for t in range(nt):
        ctx = t < nct
        tl = t - nct
        mrow.append(0 if ctx else 1 + tl // nb)
        isctx.append(1 if ctx else 0)
        cidx.append(min(t, nct - 1))
        lidx.append(max(tl, 0))
        hprev.append(0 if ctx or tl % nb == 0 else 1)
        hnext.append(0 if ctx or tl % nb == nb - 1 else 1)
        pblk.append(max(t * per - 1, 0))
        nblk.append(min((t + 1) * per, nt * per - 1))
    return [np.asarray(a, np.int32) for a in (mrow, isctx, cidx, lidx, hprev, hnext, pblk, nblk)]


def _proj_out(li, x, mod, pc, conv_w, o_f, o_b, pv, gng, hm, y_ctx, y_lat, w_out, plan):
    r = x.shape[0]
    nt = plan[0].shape[0]
    row = lambda t, *_: (t, 0)
    const = lambda t, *_: (0, 0)
    layer = lambda t, *_: (li, 0, 0)
    grid_spec = pltpu.PrefetchScalarGridSpec(
        num_scalar_prefetch=8,
        grid=(nt,),
        in_specs=[
            pl.BlockSpec((TM_OUT, D), row),
            pl.BlockSpec((None, None, 6, D), lambda t, m, *_: (li, m[t], 0, 0)),
            pl.BlockSpec((TM_OUT, PC_W), row),
            pl.BlockSpec((8, PC_W), lambda t, m, ic, ci, lidx, hp, hn, pb, nb_: (pb[t], 0)),
            pl.BlockSpec((8, PC_W), lambda t, m, ic, ci, lidx, hp, hn, pb, nb_: (nb_[t], 0)),
            pl.BlockSpec((None, 3, CONV_W), layer),
            pl.BlockSpec((TM_OUT, GVW), row),
            pl.BlockSpec((TM_OUT, GVW), row),
            pl.BlockSpec((TM_OUT, GVW), lambda t, *_: (t, 1)),
            pl.BlockSpec((None, 1, GVW), layer),
            pl.BlockSpec((GVW, GVW), const),
            pl.BlockSpec((TM_OUT, NW), lambda t, m, ic, ci, *_: (ci[t], 0)),
            pl.BlockSpec((TM_OUT, NW), lambda t, m, ic, ci, lidx, *_: (lidx[t], 0)),
            pl.BlockSpec((None, D, D), layer),
        ],
        out_specs=pl.BlockSpec((TM_OUT, D), row),
    )
    return pl.pallas_call(
        _proj_out_kernel,
        grid_spec=grid_spec,
        out_shape=jax.ShapeDtypeStruct((r, D), F32),
        compiler_params=_cparams(("arbitrary",)),
    )(*plan, x, mod, pc, pc, pc, conv_w, o_f, o_b, pv, gng, hm, y_ctx, y_lat, w_out)


def _ffn_dense_kernel(mrow, x_ref, g_ref, mod_ref, wg_ref, wu_ref, wd_ref, o_ref, h_scr, acc):
    del mrow
    f = pl.program_id(1)

    @pl.when(f == 0)
    def _():
        h_scr[...] = _norm_mod(x_ref[...], g_ref[...], mod_ref[3:4, :], mod_ref[4:5, :]).astype(BF16)
        acc[...] = jnp.zeros_like(acc)

    h = h_scr[...]
    hid = _silu(_dot(h, wg_ref[...])) * _dot(h, wu_ref[...])
    acc[...] += _dot(hid.astype(BF16), wd_ref[...])

    @pl.when(f == pl.num_programs(1) - 1)
    def _():
        o_ref[...] = x_ref[...] + mod_ref[5:6, :] * acc[...]


def _ffn_dense(li, j, tm, tf, x, g2, mod, wg, wu, wd, mrow):
    r = x.shape[0]
    nf = D_FF // tf
    grid_spec = pltpu.PrefetchScalarGridSpec(
        num_scalar_prefetch=1,
        grid=(r // tm, nf),
        in_specs=[
            pl.BlockSpec((tm, D), lambda t, f, m: (t, 0)),
            pl.BlockSpec((None, 1, D), lambda t, f, m: (li, 0, 0)),
            pl.BlockSpec((None, None, 6, D), lambda t, f, m: (li, m[t], 0, 0)),
            pl.BlockSpec((None, D, tf), lambda t, f, m: (j, 0, f)),
            pl.BlockSpec((None, D, tf), lambda t, f, m: (j, 0, f)),
            pl.BlockSpec((None, tf, D), lambda t, f, m: (j, f, 0)),
        ],
        out_specs=pl.BlockSpec((tm, D), lambda t, f, m: (t, 0)),
        scratch_shapes=[pltpu.VMEM((tm, D), BF16), pltpu.VMEM((tm, D), F32)],
    )
    return pl.pallas_call(
        _ffn_dense_kernel,
        grid_spec=grid_spec,
        out_shape=jax.ShapeDtypeStruct((r, D), F32),
        compiler_params=_cparams(("arbitrary", "arbitrary")),
    )(mrow, x, g2, mod, wg, wu, wd)


DH = D // 2
HI_MASK = 0xFFFF0000


def _pack_rows(a):
    u = pltpu.bitcast(a.astype(BF16).astype(F32), jnp.uint32)
    return (u[:, 0:DH] >> 16) | u[:, DH:D]


def _unpack_rows(u):
    lo = pltpu.bitcast(u << 16, F32)
    hi = pltpu.bitcast(u & jnp.uint32(HI_MASK), F32)
    return lo, hi


def _router_kernel(mrow, x_ref, g_ref, mod_ref, rhi_ref, rlo_ref, h_ref, ri_ref, rw_ref):
    del mrow
    h = _norm_mod(x_ref[...], g_ref[...], mod_ref[3:4, :], mod_ref[4:5, :])
    h_ref[...] = _pack_rows(h)
    hhi, hlo = _split_bf16(h)
    logits = _dot(hhi, rhi_ref[...]) + _dot(hlo, rhi_ref[...]) + _dot(hhi, rlo_ref[...])
    lane = lax.broadcasted_iota(jnp.int32, logits.shape, 1).astype(F32)
    lg = jnp.where(lane < N_EXP, logits, -jnp.inf)
    m1 = lg.max(axis=-1, keepdims=True)
    i1 = jnp.where(lg == m1, lane, 128.0).min(axis=-1, keepdims=True)
    lg2 = jnp.where(lane == i1, -jnp.inf, lg)
    m2 = lg2.max(axis=-1, keepdims=True)
    i2 = jnp.where(lg2 == m2, lane, 128.0).min(axis=-1, keepdims=True)
    e = jnp.exp(m2 - m1)
    w1 = 1.0 / (1.0 + e)
    w2 = e / (1.0 + e)
    ri_ref[...] = jnp.where(lane == 0.0, i1, jnp.where(lane == 1.0, i2, 0.0)).astype(jnp.int32)
    rw_ref[...] = jnp.where(lane == 0.0, w1, jnp.where(lane == 1.0, w2, 0.0))


def _router(li, j, x, g2, mod, rhi, rlo, mrow):
    r = x.shape[0]
    row = lambda t, m: (t, 0)
    grid_spec = pltpu.PrefetchScalarGridSpec(
        num_scalar_prefetch=1,
        grid=(r // TM_PROJ,),
        in_specs=[
            pl.BlockSpec((TM_PROJ, D), row),
            pl.BlockSpec((None, 1, D), lambda t, m: (li, 0, 0)),
            pl.BlockSpec((None, None, 6, D), lambda t, m: (li, m[t], 0, 0)),
            pl.BlockSpec((None, D, 128), lambda t, m: (j, 0, 0)),
            pl.BlockSpec((None, D, 128), lambda t, m: (j, 0, 0)),
        ],
        out_specs=[
            pl.BlockSpec((TM_PROJ, DH), row),
            pl.BlockSpec((TM_PROJ, 128), row),
            pl.BlockSpec((TM_PROJ, 128), row),
        ],
    )
    return pl.pallas_call(
        _router_kernel,
        grid_spec=grid_spec,
        out_shape=[
            jax.ShapeDtypeStruct((r, DH), jnp.uint32),
            jax.ShapeDtypeStruct((r, 128), jnp.int32),
            jax.ShapeDtypeStruct((r, 128), F32),
        ],
        compiler_params=_cparams(("arbitrary",)),
    )(mrow, x, g2, mod, rhi, rlo)


def _row_copy(src_ref, src_row, dst_ref, dst_row, sem):
    return pltpu.make_async_copy(src_ref.at[pl.ds(src_row, 1)], dst_ref.at[pl.ds(dst_row, 1)], sem)


def _dispatch_kernel(dest, pstart, plen, nused, h_ref, xg_hbm, zbuf, sem, zsem):
    t = pl.program_id(0)
    base = t * 2 * TD_DISP
    tm = zbuf.shape[0]
    n_tiles = xg_hbm.shape[0] // tm

    def start(rr, c):
        _row_copy(h_ref, rr, xg_hbm, dest[base + 2 * rr], sem).start()
        _row_copy(h_ref, rr, xg_hbm, dest[base + 2 * rr + 1], sem).start()
        return c

    lax.fori_loop(0, TD_DISP, start, 0, unroll=8)

    @pl.when(t == 0)
    def _():
        zbuf[...] = jnp.zeros_like(zbuf)
        for e in range(N_EXP):
            n = plen[e]
            s0 = pstart[e]
            head = n & 7
            pads = [(rr < head, _row_copy(zbuf, 0, xg_hbm, s0 + rr, zsem)) for rr in range(7)]
            for bit in reversed(range(3, tm.bit_length() - 1)):
                size = 1 << bit
                first = pl.multiple_of(s0 + head + ((n - head) & ~(2 * size - 1)), 8)
                pads.append(((n & size) != 0,
                             pltpu.make_async_copy(zbuf.at[pl.ds(0, size)], xg_hbm.at[pl.ds(first, size)], zsem)))
            for take, cp in pads:
                pl.when(take)(cp.start)
            for take, cp in pads:
                pl.when(take)(cp.wait)

        def tile_copy(i):
            return pltpu.make_async_copy(zbuf, xg_hbm.at[pl.ds(i * tm, tm)], zsem)

        def tstart(i, c):
            tile_copy(i).start()
            return c

        def twait(i, c):
            tile_copy(i).wait()
            return c

        lax.fori_loop(nused[0], n_tiles, tstart, 0)
        lax.fori_loop(nused[0], n_tiles, twait, 0)

    def wait(rr, c):
        _row_copy(h_ref, 0, xg_hbm, 0, sem).wait()
        return c

    lax.fori_loop(0, 2 * TD_DISP, wait, 0, unroll=8)


def _dispatch(tm, h, dest, pstart, plen, nused, n_tiles):
    r = h.shape[0]
    grid_spec = pltpu.PrefetchScalarGridSpec(
        num_scalar_prefetch=4,
        grid=(r // TD_DISP,),
        in_specs=[pl.BlockSpec((TD_DISP, DH), lambda t, *_: (t, 0))],
        out_specs=pl.BlockSpec(memory_space=pl.ANY),
        scratch_shapes=[pltpu.VMEM((tm, DH), jnp.uint32), pltpu.SemaphoreType.DMA(()),
                        pltpu.SemaphoreType.DMA(())],
    )
    return pl.pallas_call(
        _dispatch_kernel,
        grid_spec=grid_spec,
        out_shape=jax.ShapeDtypeStruct((n_tiles * tm, DH), jnp.uint32),
        compiler_params=_cparams(("arbitrary",)),
    )(dest, pstart, plen, nused, h)


def _ffn_grouped_kernel(te, nused, x_ref, wg_ref, wu_ref, wd_ref, o_ref, xb, acc):
    del te
    i = pl.program_id(0)
    f = pl.program_id(1)
    last = pl.num_programs(1) - 1
    used = i < nused[0]

    @pl.when(jnp.logical_and(used, f == 0))
    def _():
        lo, hi = _unpack_rows(x_ref[...])
        xb[:, 0:DH] = lo.astype(BF16)
        xb[:, DH:D] = hi.astype(BF16)
        acc[...] = jnp.zeros_like(acc)

    @pl.when(used)
    def _():
        h = xb[...]
        hid = _silu(_dot(h, wg_ref[...].astype(BF16))) * _dot(h, wu_ref[...].astype(BF16))
        acc[...] += _dot(hid.astype(BF16), wd_ref[...].astype(BF16))

    @pl.when(jnp.logical_and(used, f == last))
    def _():
        o_ref[...] = _pack_rows(acc[...])

    @pl.when(jnp.logical_and(jnp.logical_not(used), f == last))
    def _():
        o_ref[...] = jnp.zeros_like(o_ref)


def _ffn_grouped(j, tm, tf, xg, wg, wu, wd, te, nused, n_tiles):
    nf = D_EXP // tf
    fidx = lambda i, f, n: jnp.where(i < n[0], f, nf - 1)
    grid_spec = pltpu.PrefetchScalarGridSpec(
        num_scalar_prefetch=2,
        grid=(n_tiles, nf),
        in_specs=[
            pl.BlockSpec((tm, DH), lambda i, f, e, n: (jnp.minimum(i, n[0] - 1), 0)),
            pl.BlockSpec((None, None, D, tf), lambda i, f, e, n: (j, e[i], 0, fidx(i, f, n))),
            pl.BlockSpec((None, None, D, tf), lambda i, f, e, n: (j, e[i], 0, fidx(i, f, n))),
            pl.BlockSpec((None, None, tf, D), lambda i, f, e, n: (j, e[i], fidx(i, f, n), 0)),
        ],
        out_specs=pl.BlockSpec((tm, DH), lambda i, f, e, n: (i, 0)),
        scratch_shapes=[pltpu.VMEM((tm, D), BF16), pltpu.VMEM((tm, D), F32)],
    )
    return pl.pallas_call(
        _ffn_grouped_kernel,
        grid_spec=grid_spec,
        out_shape=jax.ShapeDtypeStruct((n_tiles * tm, DH), jnp.uint32),
        compiler_params=_cparams(("arbitrary", "arbitrary")),
    )(te, nused, xg, wg, wu, wd)


def _combine_kernel(mrow, dest, x_ref, mod_ref, rw_ref, y_hbm, o_ref, buf, sem):
    del mrow
    t = pl.program_id(0)
    base = t * TC_COMB * 2

    def start(rr, c):
        _row_copy(y_hbm, dest[base + rr], buf, rr, sem).start()
        return c

    lax.fori_loop(0, 2 * TC_COMB, start, 0, unroll=8)

    def wait(rr, c):
        _row_copy(y_hbm, 0, buf, 0, sem).wait()
        return c

    lax.fori_loop(0, 2 * TC_COMB, wait, 0, unroll=8)
    rw = rw_ref[...]
    lo0, hi0 = _unpack_rows(buf[0:TC_COMB, :])
    lo1, hi1 = _unpack_rows(buf[TC_COMB:2 * TC_COMB, :])
    w0, w1 = rw[:, 0:1], rw[:, 1:2]
    o_ref[:, 0:DH] = x_ref[:, 0:DH] + mod_ref[5:6, 0:DH] * (w0 * lo0 + w1 * lo1)
    o_ref[:, DH:D] = x_ref[:, DH:D] + mod_ref[5:6, DH:D] * (w0 * hi0 + w1 * hi1)


def _combine(li, x, mod, rw, yg, dest, mrow):
    r = x.shape[0]
    grid_spec = pltpu.PrefetchScalarGridSpec(
        num_scalar_prefetch=2,
        grid=(r // TC_COMB,),
        in_specs=[
            pl.BlockSpec((TC_COMB, D), lambda t, m, d: (t, 0)),
            pl.BlockSpec((None, None, 6, D), lambda t, m, d: (li, m[t], 0, 0)),
            pl.BlockSpec((TC_COMB, 128), lambda t, m, d: (t, 0)),
            pl.BlockSpec(memory_space=pl.ANY),
        ],
        out_specs=pl.BlockSpec((TC_COMB, D), lambda t, m, d: (t, 0)),
        scratch_shapes=[pltpu.VMEM((2 * TC_COMB, DH), jnp.uint32), pltpu.SemaphoreType.DMA(())],
    )
    return pl.pallas_call(
        _combine_kernel,
        grid_spec=grid_spec,
        out_shape=jax.ShapeDtypeStruct((r, D), F32),
        compiler_params=_cparams(("arbitrary",)),
    )(mrow, dest, x, mod, rw, yg)


def _moe_plan(ri, tm, n_tiles):
    r = ri.shape[0]
    ef = ri[:, :2].reshape(-1)
    oh = (ef[:, None] == jnp.arange(N_EXP, dtype=jnp.int32)[None, :]).astype(jnp.int32)
    csum = jnp.cumsum(oh, axis=0)
    pos = jnp.sum(csum * oh, axis=1) - 1
    counts = csum[-1]
    tiles = (counts + tm - 1) // tm
    tile_end = jnp.cumsum(tiles)
    off = (tile_end - tiles) * tm
    dest = (jnp.sum(off[None, :] * oh, axis=1) + pos).astype(jnp.int32)
    tile_id = jnp.arange(n_tiles, dtype=jnp.int32)
    te = jnp.minimum(jnp.sum((tile_end[None, :] <= tile_id[:, None]).astype(jnp.int32), axis=1), N_EXP - 1)
    nused = tile_end[-1:].astype(jnp.int32)
    pstart = (off + counts).astype(jnp.int32)
    plen = (tiles * tm - counts).astype(jnp.int32)
    dest_t = dest.reshape(r // TC_COMB, TC_COMB, 2).transpose(0, 2, 1).reshape(-1)
    return dest, dest_t, te.astype(jnp.int32), nused, pstart, plen


def _mod_rows(n_ctx, n_lat, tile):
    rows = [0] * (n_ctx * SEQ // tile)
    for b in range(n_lat):
        rows += [1 + b] * (TLAT // tile)
    return np.asarray(rows, np.int32)


def _rope_tables():
    t = np.arange(TLAT)
    lane = np.arange(GWP)
    p = lane % GK
    sub = p % (GK // 2)
    nf = GK // 4
    freq = ROPE_BASE ** (-(sub % nf).astype(np.float32) / nf)
    pos = np.where((p < GK // 2)[None, :], (t // GRID_W)[:, None], (t % GRID_W)[:, None]).astype(np.float32)
    ang = jnp.asarray(pos) * jnp.asarray(freq.astype(np.float32))[None, :]
    cos, sin = jnp.cos(ang), jnp.sin(ang)
    lowhalf = jnp.asarray((sub < nf)[None, :])
    sina = jnp.where(lowhalf, -sin, 0.0)
    sinb = jnp.where(lowhalf, 0.0, sin)
    pad1 = jnp.ones((TM_PROJ, GWP), F32)
    pad0 = jnp.zeros((TM_PROJ, GWP), F32)
    return (jnp.concatenate([cos, pad1]), jnp.concatenate([sina, pad0]), jnp.concatenate([sinb, pad0]))


def _nat_bias_blocks(rpb):
    nl = rpb.shape[0]
    col = np.arange(GRID_W)
    c0 = np.clip(col - NAT_KW // 2, 0, GRID_W - NAT_KW)
    in_win = (col[None, :] >= c0[:, None]) & (col[None, :] < c0[:, None] + NAT_KW)
    dc = np.clip(col[None, :] - col[:, None], -(NAT_KW - 1), NAT_KW - 1) + NAT_KW - 1
    onehot = np.zeros((2 * NAT_KW - 1, GRID_W, GRID_W), np.float32)
    onehot[dc, col[:, None], col[None, :]] = 1.0
    sel = jnp.einsum("lhad,dqk->lhaqk", rpb, jnp.asarray(onehot), precision=lax.Precision.HIGHEST)
    cm = jnp.where(jnp.asarray(in_win)[None, None, None], sel, NEG)
    na = 2 * NAT_KH - 1
    neg = jnp.full((nl, NH, NAT_KH, GRID_W, GRID_W), NEG, F32)
    full = jnp.concatenate([cm[:, :, 0:na - 1], cm[:, :, 1:na]], axis=-1)
    left = jnp.concatenate([neg, cm[:, :, 0:NAT_KH]], axis=-1)
    right = jnp.concatenate([cm[:, :, NAT_KH - 1:na], neg], axis=-1)
    none = jnp.concatenate([neg[:, :, 0:1], neg[:, :, 0:1]], axis=-1)
    return jnp.concatenate([full, left, right, none], axis=2)


def _heads_to_rows(a):
    b, h, t, d = a.shape
    return a.transpose(0, 2, 1, 3).reshape(b, t, h * d)


def _rows_to_heads(a, b, t):
    return a.reshape(b, t, NH, HD).transpose(0, 2, 1, 3)


def _state_to_blockdiag(s):
    b = s.shape[0]
    eye = jnp.eye(GH, dtype=s.dtype)
    bd = jnp.einsum("bhkv,hg->bhvgk", s, eye).reshape(b, GVW, GW)
    return jnp.pad(bd, ((0, 0), (0, 0), (0, GWP - GW)))


def _blockdiag_to_state(st):
    b = st.shape[0]
    s5 = st[:, :, :GW].reshape(b, GH, GV, GH, GK)
    return jnp.stack([s5[:, h, :, h, :] for h in range(GH)], axis=1).transpose(0, 1, 3, 2)


def kernel(x_prompt, x_sample, cache_nat_k, cache_nat_v, state_gla, c, c_ctx, norm1_g, norm2_g, w_mod, b_mod, w_in, conv_w, gla_gate_w2, gla_gate_b, gla_norm_g, nat_q_norm_g, nat_k_norm_g, nat_rpb, w_out, ffn_w_gate, ffn_w_up, ffn_w_down, moe_router, moe_w_gate, moe_w_up, moe_w_down):
    n_ctx, n_lat = x_prompt.shape[0], x_sample.shape[0]
    depth = w_in.shape[0]
    ncr = n_ctx * SEQ
    r = ncr + n_lat * TLAT
    assert ncr % TLAT == 0 and n_lat + 1 <= 8

    x = jnp.concatenate([x_prompt.reshape(ncr, D), x_sample.reshape(n_lat * TLAT, D)], axis=0)
    cvecs = jnp.zeros((8, D), F32).at[0].set(c_ctx).at[1:1 + n_lat].set(c)
    mod = _modulation(cvecs, w_mod, b_mod)

    z = lambda n: jnp.zeros((depth, D, n), F32)
    w_in_p = jnp.concatenate([
        w_in[:, :, 768:960], z(GWP - GW), w_in[:, :, 960:1152], z(GWP - GW),
        w_in[:, :, 1920:1952], z(128 - 2 * LR),
        w_in[:, :, 0:768], w_in[:, :, 1152:1920], w_in[:, :, 1952:3104]], axis=-1).astype(BF16)
    w_out_b = w_out.astype(BF16)
    w2 = jnp.pad(gla_gate_w2, ((0, 0), (0, 0), (0, 0), (0, GWP - GW)))
    w2f = jnp.pad(w2[:, 0], ((0, 0), (0, 128 - LR), (0, 0))).astype(BF16)
    w2b = jnp.pad(w2[:, 1], ((0, 0), (LR, 128 - 2 * LR), (0, 0))).astype(BF16)
    gb = jnp.pad(gla_gate_b, ((0, 0), (0, 0), (0, GWP - GW)))[:, :, None, :]
    hm = jnp.asarray(np.kron(np.eye(NH), np.full((HD, HD), 1.0 / HD)), BF16)
    g1 = norm1_g[:, None, :]
    g2 = norm2_g[:, None, :]
    qg = jnp.tile(nat_q_norm_g, (1, NH))[:, None, :]
    kg = jnp.tile(nat_k_norm_g, (1, NH))[:, None, :]
    gng = jnp.tile(gla_norm_g, (1, GH))[:, None, :]
    gla_masks = _gla_masks()
    rope = _rope_tables()
    pb = _nat_bias_blocks(nat_rpb)
    ck = _heads_to_rows(cache_nat_k.transpose(1, 0, 2, 3, 4).reshape(depth * n_lat, NH, SEQ, HD))
    cv = _heads_to_rows(cache_nat_v.transpose(1, 0, 2, 3, 4).reshape(depth * n_lat, NH, SEQ, HD))
    ck = ck.reshape(depth, n_lat, SEQ, NW).astype(BF16)
    cv = cv.reshape(depth, n_lat, SEQ, NW).astype(BF16)
    ffn_g, ffn_u, ffn_d = ffn_w_gate.astype(BF16), ffn_w_up.astype(BF16), ffn_w_down.astype(BF16)
    router_p = jnp.pad(moe_router, ((0, 0), (0, 0), (0, 128 - N_EXP)))
    rhi = router_p.astype(BF16)
    rlo = (router_p - rhi.astype(F32)).astype(BF16)

    mrow_proj = jnp.asarray(_mod_rows(n_ctx, n_lat, TM_PROJ))
    in_plan = [jnp.asarray(a) for a in _proj_in_plan(n_ctx, n_lat)]
    mrow_comb = jnp.asarray(_mod_rows(n_ctx, n_lat, TC_COMB))
    gla_plan = [jnp.asarray(a) for a in _gla_plan(n_ctx, n_lat)]
    out_plan = [jnp.asarray(a) for a in _proj_out_plan(n_ctx, n_lat)]

    new_k, new_v, new_s = [], [], []
    for i in range(depth):
        pg, pc, pv, pn, knf, vnf = _proj_in(i, x, g1, mod, w_in_p, hm, qg, kg, rope, w2f, w2b, gb,
                                            in_plan, ncr)
        s0f = _state_to_blockdiag(state_gla[:, i, 0])
        s0b = _state_to_blockdiag(state_gla[:, i, 1])
        o_f, o_b, sff, sfb = _gla(pg, pv, gla_masks, s0f, s0b, gla_plan, n_ctx)
        y_ctx = _ctx_attn(pn, n_ctx)
        y_lat = _nat_attn(i, pn, ck, cv, pb, ncr, n_lat)
        x = _proj_out(i, x, mod, pc, conv_w, o_f, o_b, pv, gng, hm, y_ctx, y_lat, w_out_b, out_plan)
        j = i // 2
        if i % 2 == 0:
            tm, tf = FFN_TILE
            mrow_ffn = jnp.asarray(_mod_rows(n_ctx, n_lat, tm))
            x = _ffn_dense(i, j, tm, tf, x, g2, mod, ffn_g, ffn_u, ffn_d, mrow_ffn)
        else:
            tm, tf = MOE_TILE
            n_tiles = 2 * r // tm + N_EXP
            h, ri, rw = _router(i, j, x, g2, mod, rhi, rlo, mrow_proj)
            dest, dest_t, te, nused, pstart, plen = _moe_plan(ri, tm, n_tiles)
            xg = _dispatch(tm, h, dest, pstart, plen, nused, n_tiles)
            yg = _ffn_grouped(j, tm, tf, xg, moe_w_gate, moe_w_up, moe_w_down, te, nused, n_tiles)
            x = _combine(i, x, mod, rw, yg, dest_t, mrow_comb)
        new_k.append(_rows_to_heads(knf, n_ctx, SEQ))
        new_v.append(_rows_to_heads(vnf, n_ctx, SEQ))
        new_s.append(jnp.stack([_blockdiag_to_state(sff), _blockdiag_to_state(sfb)], axis=1))

    y_prompt = x[:ncr].reshape(n_ctx, SEQ, D)
    y_sample = x[ncr:].reshape(n_lat, TLAT, D)
    return (y_prompt, y_sample, jnp.stack(new_k, axis=1), jnp.stack(new_v, axis=1), jnp.stack(new_s, axis=1))
```

```python
import numpy as np
import jax
import jax.numpy as jnp
from jax import lax
from jax.experimental import pallas as pl
from jax.experimental.pallas import tpu as pltpu

F32 = jnp.float32
BF16 = jnp.bfloat16

D = 1024
SEQ = 256
TLAT = 4096
GRID_W = 64
HD = 64
CONV_W = 256
GH, GK, GV = 6, 32, 64
NH = 6
LR = 16
GATE_NORM = 16.0
CHUNK = 64
NAT_KH, NAT_KW = 8, 16
ROPE_BASE = 10000.0
D_FF = 2816
N_EXP = 8
D_EXP = 3584
EPS = 1e-6
NEG = -1e30

GW = GH * GK
GWP = 256
GVW = GH * GV
NW = NH * HD

PG_IN = 640
PG_W = 1024
PC_W = 768
PV_W = 768
PN_W = 1152
W_IN_P = PG_IN + PC_W + PV_W + PN_W

TM_PROJ = 512
BLK = 256
TM_OUT = 512
TQ = 512
BAND = 1024
FFN_TILE = (512, 1408)
MOE_TILE = (1024, 512)
TC_COMB = 256
TD_DISP = 256

VMEM_LIMIT = 56 * 1024 * 1024


def _cparams(sem):
    return pltpu.CompilerParams(dimension_semantics=sem, vmem_limit_bytes=VMEM_LIMIT)


def _dot(a, b):
    return jnp.dot(a, b, preferred_element_type=F32)


def _dot_nt(a, b):
    return lax.dot_general(a, b, (((1,), (1,)), ((), ())), preferred_element_type=F32)


def _dot_tn(a, b):
    return lax.dot_general(a, b, (((0,), (0,)), ((), ())), preferred_element_type=F32)


def _split_bf16(a):
    hi = a.astype(BF16)
    lo = (a - hi.astype(F32)).astype(BF16)
    return hi, lo


def _silu(a):
    return a * jax.nn.sigmoid(a)


def _head_mean(sq, hm):
    hi, lo = _split_bf16(sq)
    return _dot(hi, hm) + _dot(lo, hm)


def _mod_kernel(c_ref, w_ref, b_ref, o_ref):
    s = _silu(c_ref[...])
    o_ref[...] = _dot(s.astype(BF16), w_ref[...].astype(BF16)) + b_ref[...]


def _modulation(cvecs, w_mod, b_mod):
    nl = w_mod.shape[0]
    out = pl.pallas_call(
        _mod_kernel,
        grid=(nl, 6),
        in_specs=[
            pl.BlockSpec((8, D), lambda l, j: (0, 0)),
            pl.BlockSpec((None, D, D), lambda l, j: (l, 0, j)),
            pl.BlockSpec((None, 1, D), lambda l, j: (l, 0, j)),
        ],
        out_specs=pl.BlockSpec((None, 8, D), lambda l, j: (l, 0, j)),
        out_shape=jax.ShapeDtypeStruct((nl, 8, 6 * D), F32),
        compiler_params=_cparams(("arbitrary", "arbitrary")),
    )(cvecs, w_mod, b_mod.reshape(nl, 1, 6 * D))
    return out.reshape(nl, 8, 6, D)


def _norm_mod(x, g, shift, scale):
    ms = jnp.mean(x * x, axis=-1, keepdims=True)
    return x * lax.rsqrt(ms + EPS) * g * (1.0 + scale) + shift


def _log_sigmoid(z):
    return jnp.minimum(z, 0.0) - jnp.log1p(jnp.exp(-jnp.abs(z)))


def _rope(a, cos, sina, sinb):
    return a * cos + pltpu.roll(a, GWP - 8, 1) * sina + pltpu.roll(a, 8, 1) * sinb


def _proj_in_kernel(rtile, mrow, tbl, kvblk, x_ref, g_ref, mod_ref, w_ref, hm_ref, qg_ref, kg_ref,
                    cos_ref, sina_ref, sinb_ref, w2f_ref, w2b_ref, gbf_ref, gbb_ref,
                    pg_ref, pc_ref, pv_ref, pn_ref, knf_ref, vnf_ref):
    del rtile, mrow, tbl, kvblk
    h = _norm_mod(x_ref[...], g_ref[...], mod_ref[0:1, :], mod_ref[1:2, :]).astype(BF16)
    gq = _dot(h, w_ref[:, 0:GWP]) * (GK ** -0.5)
    gk = _dot(h, w_ref[:, GWP:2 * GWP])
    lr = _dot(h, w_ref[:, 2 * GWP:PG_IN]).astype(BF16)
    cos, sina, sinb = cos_ref[...], sina_ref[...], sinb_ref[...]
    pg_ref[:, 0:GWP] = _rope(gq, cos, sina, sinb).astype(BF16)
    pg_ref[:, GWP:2 * GWP] = _rope(gk, cos, sina, sinb).astype(BF16)
    gf = _log_sigmoid(_dot(lr, w2f_ref[...]) + gbf_ref[...]) * (1.0 / GATE_NORM)
    gb = _log_sigmoid(_dot(lr, w2b_ref[...]) + gbb_ref[...]) * (1.0 / GATE_NORM)
    pg_ref[:, 2 * GWP:3 * GWP] = gf.astype(BF16)
    pg_ref[:, 3 * GWP:4 * GWP] = gb.astype(BF16)
    pc_ref[...] = _dot(h, w_ref[:, PG_IN:PG_IN + PC_W]).astype(BF16)
    pv_ref[...] = _dot(h, w_ref[:, PG_IN + PC_W:PG_IN + PC_W + PV_W]).astype(BF16)
    o = PG_IN + PC_W + PV_W
    nq = _dot(h, w_ref[:, o:o + NW])
    nk = _dot(h, w_ref[:, o + NW:o + 2 * NW])
    nv = _dot(h, w_ref[:, o + 2 * NW:o + 3 * NW])
    hm = hm_ref[...]
    qn = nq * lax.rsqrt(_head_mean(nq * nq, hm) + EPS) * qg_ref[...]
    kn = nk * lax.rsqrt(_head_mean(nk * nk, hm) + EPS) * kg_ref[...]
    pn_ref[:, 0:NW] = (qn * (HD ** -0.5)).astype(BF16)
    pn_ref[:, NW:2 * NW] = kn.astype(BF16)
    pn_ref[:, 2 * NW:3 * NW] = nv.astype(BF16)
    knf_ref[...] = kn
    vnf_ref[...] = nv


def _proj_in_plan(n_ctx, n_lat):
    nct = n_ctx * SEQ // TM_PROJ
    per = TLAT // TM_PROJ
    rtile, mrow, tbl, kvblk = [], [], [], []
    for b in range(n_lat):
        for j in range(per):
            rtile.append(nct + b * per + j); mrow.append(1 + b); tbl.append(j); kvblk.append(0)
    for t in range(nct):
        rtile.append(t); mrow.append(0); tbl.append(per); kvblk.append(t)
    return [np.asarray(a, np.int32) for a in (rtile, mrow, tbl, kvblk)]


def _proj_in(li, x, g1, mod, w_in_p, hm, qg, kg, rope, w2f, w2b, gb, plan, n_ctx_rows):
    r = x.shape[0]
    nt = r // TM_PROJ
    cos, sina, sinb = rope
    row = lambda t, rt, *_: (rt[t], 0)
    const = lambda t, *_: (0, 0)
    layer = lambda t, *_: (li, 0, 0)
    tab = pl.BlockSpec((TM_PROJ, GWP), lambda t, rt, m, tb, kv: (tb[t], 0))
    kvo = pl.BlockSpec((TM_PROJ, NW), lambda t, rt, m, tb, kv: (kv[t], 0))
    grid_spec = pltpu.PrefetchScalarGridSpec(
        num_scalar_prefetch=4,
        grid=(nt,),
        in_specs=[
            pl.BlockSpec((TM_PROJ, D), row),
            pl.BlockSpec((None, 1, D), layer),
            pl.BlockSpec((None, None, 6, D), lambda t, rt, m, *_: (li, m[t], 0, 0)),
            pl.BlockSpec((None, D, W_IN_P), layer),
            pl.BlockSpec((NW, NW), const),
            pl.BlockSpec((None, 1, NW), layer),
            pl.BlockSpec((None, 1, NW), layer),
            tab, tab, tab,
            pl.BlockSpec((None, 128, GWP), layer), pl.BlockSpec((None, 128, GWP), layer),
            pl.BlockSpec((None, None, 1, GWP), lambda t, *_: (li, 0, 0, 0)),
            pl.BlockSpec((None, None, 1, GWP), lambda t, *_: (li, 1, 0, 0)),
        ],
        out_specs=[
            pl.BlockSpec((TM_PROJ, PG_W), row),
            pl.BlockSpec((TM_PROJ, PC_W), row),
            pl.BlockSpec((TM_PROJ, PV_W), row),
            pl.BlockSpec((TM_PROJ, PN_W), row),
            kvo, kvo,
        ],
    )
    return pl.pallas_call(
        _proj_in_kernel,
        grid_spec=grid_spec,
        out_shape=[
            jax.ShapeDtypeStruct((r, PG_W), BF16),
            jax.ShapeDtypeStruct((r, PC_W), BF16),
            jax.ShapeDtypeStruct((r, PV_W), BF16),
            jax.ShapeDtypeStruct((r, PN_W), BF16),
            jax.ShapeDtypeStruct((n_ctx_rows, NW), F32),
            jax.ShapeDtypeStruct((n_ctx_rows, NW), F32),
        ],
        compiler_params=_cparams(("arbitrary",)),
    )(*plan, x, g1, mod, w_in_p, hm, qg, kg, cos, sina, sinb, w2f, w2b, gb, gb)


def _gla_direction(pg, v, tri, km, vm, am, sm, st_ref, o_ref, rev):
    q = pg[:, 0:GWP].astype(F32)
    k = pg[:, GWP:2 * GWP].astype(F32)
    g = pg[:, 3 * GWP:4 * GWP] if rev else pg[:, 2 * GWP:3 * GWP]
    cum = _dot(tri, g)
    qd = (q * jnp.exp(cum)).astype(BF16)
    kd = (k * jnp.exp(-cum)).astype(BF16)
    chunks = range(BLK // CHUNK)
    for c in (reversed(chunks) if rev else chunks):
        lo = c * CHUNK
        sl = slice(lo, lo + CHUNK)
        edge = lo if rev else lo + CHUNK - 1
        cend = cum[edge:edge + 1, :]
        kst = (k[sl] * jnp.exp(cend - cum[sl])).astype(BF16)
        decay = jnp.exp(cend)
        kblk = jnp.concatenate([kd[sl]] * GH, axis=0) * km
        a = jnp.where(am > 0.0, _dot_nt(qd[sl], kblk), 0.0).astype(BF16)
        v_c = v[sl]
        vblk = jnp.concatenate([v_c] * GH, axis=0) * vm
        st = st_ref[...]
        o = _dot(a, vblk) + _dot_nt(qd[sl], st.astype(BF16))
        o_ref[sl, :] = o.astype(o_ref.dtype)
        ut = _dot_tn(v_c, kst)
        st_ref[...] = st * decay + ut * sm


def _gla_kernel(fblk, bblk, sidx, oidx, first, isctx,
                pgf_ref, pvf_ref, pgb_ref, pvb_ref, trif_ref, trib_ref, km_ref, vm_ref,
                amf_ref, amb_ref, sm_ref, s0f_ref, s0b_ref,
                of_ref, ob_ref, sff_ref, sfb_ref, stf, stb):
    del fblk, bblk, sidx, oidx
    u = pl.program_id(0)

    @pl.when(first[u] == 1)
    def _():
        ctx = isctx[u] == 1
        stf[...] = jnp.where(ctx, 0.0, s0f_ref[...])
        stb[...] = jnp.where(ctx, 0.0, s0b_ref[...])

    km, vm, sm = km_ref[...], vm_ref[...], sm_ref[...]
    _gla_direction(pgf_ref[...], pvf_ref[...], trif_ref[...], km, vm, amf_ref[...], sm, stf, of_ref, False)
    _gla_direction(pgb_ref[...], pvb_ref[...], trib_ref[...], km, vm, amb_ref[...], sm, stb, ob_ref, True)
    sff_ref[...] = stf[...]
    sfb_ref[...] = stb[...]


def _gla_plan(n_ctx, n_lat):
    nb = TLAT // BLK
    fblk, bblk, sidx, oidx, first, isctx = [], [], [], [], [], []
    for s in range(n_lat):
        for j in range(nb):
            fblk.append(n_ctx + s * nb + j)
            bblk.append(n_ctx + s * nb + nb - 1 - j)
            sidx.append(s); oidx.append(0); first.append(1 if j == 0 else 0); isctx.append(0)
    for s in range(n_ctx):
        fblk.append(s); bblk.append(s); sidx.append(0); oidx.append(s); first.append(1); isctx.append(1)
    return [np.asarray(a, np.int32) for a in (fblk, bblk, sidx, oidx, first, isctx)]


def _gla_masks():
    ii = np.arange(BLK)
    same = (ii[:, None] // CHUNK) == (ii[None, :] // CHUNK)
    trif = same & (ii[None, :] <= ii[:, None])
    trib = same & (ii[None, :] >= ii[:, None])
    rk = np.arange(GVW)[:, None] // CHUNK
    km = rk == (np.arange(GWP)[None, :] // GK)
    vm = rk == (np.arange(GVW)[None, :] // GV)
    t = np.arange(CHUNK)[:, None]
    j = np.arange(GVW)[None, :] % CHUNK
    return (jnp.asarray(trif, BF16), jnp.asarray(trib, BF16), jnp.asarray(km, BF16), jnp.asarray(vm, BF16),
            jnp.asarray(j <= t, F32), jnp.asarray(j >= t, F32), jnp.asarray(km, F32))


def _gla(pg, pv, masks, s0f, s0b, plan, n_ctx):
    r = pg.shape[0]
    nsteps = plan[0].shape[0]
    fb = lambda u, f, b, *_: (f[u], 0)
    bb = lambda u, f, b, *_: (b[u], 0)
    const = lambda u, *_: (0, 0)
    sq = lambda u, f, b, s, *_: (s[u], 0, 0)
    oq = lambda u, f, b, s, o, *_: (o[u], 0, 0)
    grid_spec = pltpu.PrefetchScalarGridSpec(
        num_scalar_prefetch=6,
        grid=(nsteps,),
        in_specs=[
            pl.BlockSpec((BLK, PG_W), fb), pl.BlockSpec((BLK, GVW), fb),
            pl.BlockSpec((BLK, PG_W), bb), pl.BlockSpec((BLK, GVW), bb),
            pl.BlockSpec((BLK, BLK), const), pl.BlockSpec((BLK, BLK), const),
            pl.BlockSpec((GVW, GWP), const), pl.BlockSpec((GVW, GVW), const),
            pl.BlockSpec((CHUNK, GVW), const), pl.BlockSpec((CHUNK, GVW), const),
            pl.BlockSpec((GVW, GWP), const),
            pl.BlockSpec((None, GVW, GWP), sq), pl.BlockSpec((None, GVW, GWP), sq),
        ],
        out_specs=[
            pl.BlockSpec((BLK, GVW), fb), pl.BlockSpec((BLK, GVW), bb),
            pl.BlockSpec((None, GVW, GWP), oq), pl.BlockSpec((None, GVW, GWP), oq),
        ],
        scratch_shapes=[pltpu.VMEM((GVW, GWP), F32), pltpu.VMEM((GVW, GWP), F32)],
    )
    return pl.pallas_call(
        _gla_kernel,
        grid_spec=grid_spec,
        out_shape=[
            jax.ShapeDtypeStruct((r, GVW), BF16), jax.ShapeDtypeStruct((r, GVW), BF16),
            jax.ShapeDtypeStruct((n_ctx, GVW, GWP), F32), jax.ShapeDtypeStruct((n_ctx, GVW, GWP), F32),
        ],
        compiler_params=_cparams(("arbitrary",)),
    )(*plan, pg, pv, pg, pv, *masks, s0f, s0b)


def _pair_attention(q, keys, vals, biases):
    lane = lax.broadcasted_iota(jnp.int32, (1, 2 * HD), 1)
    first = lane < HD
    outs = []
    for half in range(2):
        qm = jnp.where(first if half == 0 else jnp.logical_not(first), q, jnp.zeros_like(q))
        ss = []
        for kk, bias in zip(keys, biases):
            s = _dot_nt(qm, kk)
            if bias is not None:
                s = s + bias[half]
            ss.append(s)
        m = ss[0].max(axis=-1, keepdims=True)
        for s in ss[1:]:
            m = jnp.maximum(m, s.max(axis=-1, keepdims=True))
        acc = None
        den = None
        for s, vv in zip(ss, vals):
            e = jnp.exp(s - m)
            d = e.sum(axis=-1, keepdims=True)
            o = _dot(e.astype(BF16), vv)
            acc = o if acc is None else acc + o
            den = d if den is None else den + d
        outs.append(acc / den)
    return jnp.where(first, outs[0], outs[1])


def _ctx_attn_kernel(q_ref, k_ref, v_ref, o_ref):
    for p in range(NH // 2):
        sl = slice(p * 2 * HD, (p + 1) * 2 * HD)
        o = _pair_attention(q_ref[:, sl], [k_ref[:, sl]], [v_ref[:, sl]], [None])
        o_ref[:, sl] = o.astype(o_ref.dtype)


def _ctx_attn(pn, n_ctx):
    return pl.pallas_call(
        _ctx_attn_kernel,
        grid=(n_ctx,),
        in_specs=[
            pl.BlockSpec((SEQ, NW), lambda b: (b, 0)),
            pl.BlockSpec((SEQ, NW), lambda b: (b, 1)),
            pl.BlockSpec((SEQ, NW), lambda b: (b, 2)),
        ],
        out_specs=pl.BlockSpec((SEQ, NW), lambda b: (b, 0)),
        out_shape=jax.ShapeDtypeStruct((n_ctx * SEQ, NW), BF16),
        compiler_params=_cparams(("arbitrary",)),
    )(pn, pn, pn)


QROWS = TQ // GRID_W
KPAIRS = BAND // (2 * GRID_W)
N_BIAS_BLK = 31


def _nat_block_table():
    rows = TLAT // GRID_W
    krows = BAND // GRID_W
    tbl = np.zeros((3, QROWS, KPAIRS), np.int32)
    for ty, (row0, ub) in enumerate(((0, 0), (QROWS, QROWS - krows // 4), (rows - QROWS, rows - krows))):
        for a in range(QROWS):
            qr = row0 + a
            bs = min(max(qr - NAT_KH // 2, 0), rows - NAT_KH)
            for kp in range(KPAIRS):
                kr0 = ub + 2 * kp
                v0 = bs <= kr0 < bs + NAT_KH
                v1 = bs <= kr0 + 1 < bs + NAT_KH
                a0 = kr0 - qr + NAT_KH - 1
                if v0 and v1:
                    tbl[ty, a, kp] = a0
                elif v1:
                    tbl[ty, a, kp] = 14 + a0 + 1
                elif v0:
                    tbl[ty, a, kp] = 22 + a0 - (NAT_KH - 1)
                else:
                    tbl[ty, a, kp] = N_BIAS_BLK - 1
    return tbl


_NAT_TBL = _nat_block_table()


def _nat_tile(win, idx_fn, q_ref, kb, vb, ck, cv, pb_ref, o_ref, s_scr, c_scr, e_scr, ec_scr):
    lane = lax.broadcasted_iota(jnp.int32, (1, 2 * HD), 1)
    first = lane < HD
    q = q_ref[...]
    w = 2 * GRID_W
    outs = []
    for half in range(2):
        qm = jnp.where(first if half == 0 else jnp.logical_not(first), q, jnp.zeros_like(q))
        s_scr[...] = _dot_nt(qm, kb)
        c_scr[...] = _dot_nt(qm, ck)
        dens = []
        for qr in range(QROWS):
            rows = slice(qr * GRID_W, (qr + 1) * GRID_W)
            lo, hi = win[qr]
            blocks = [s_scr[rows, kp * w:(kp + 1) * w] + pb_ref[half, idx_fn(qr, kp)]
                      for kp in range(lo, hi)]
            c0, c1 = c_scr[rows, 0:w], c_scr[rows, w:2 * w]
            mm = jnp.maximum(c0, c1)
            for b in blocks:
                mm = jnp.maximum(mm, b)
            m = mm.max(axis=-1, keepdims=True)
            e0, e1 = jnp.exp(c0 - m), jnp.exp(c1 - m)
            ec_scr[rows, 0:w] = e0.astype(BF16)
            ec_scr[rows, w:2 * w] = e1.astype(BF16)
            acc = e0 + e1
            for kp in range(KPAIRS):
                if lo <= kp < hi:
                    e = jnp.exp(blocks[kp - lo] - m)
                    acc = acc + e
                    e_scr[rows, kp * w:(kp + 1) * w] = e.astype(BF16)
                else:
                    e_scr[rows, kp * w:(kp + 1) * w] = jnp.zeros((GRID_W, w), BF16)
            dens.append(acc.sum(axis=-1, keepdims=True))
        o = _dot(e_scr[...], vb) + _dot(ec_scr[...], cv)
        outs.append(o / jnp.concatenate(dens, axis=0))
    o_ref[...] = jnp.where(first, outs[0], outs[1]).astype(o_ref.dtype)


def _nat_kernel(tbl, q_ref, k_ref, v_ref, ck_ref, cv_ref, pb_ref, o_ref, s_scr, c_scr, e_scr, ec_scr):
    j = pl.program_id(2)
    nj = pl.num_programs(2)
    start = pl.multiple_of(jnp.clip(j * TQ - BAND // 4, 0, TLAT - BAND), 256)
    kb = k_ref[pl.ds(start, BAND), :]
    vb = v_ref[pl.ds(start, BAND), :]
    args = (q_ref, kb, vb, ck_ref[...], cv_ref[...], pb_ref, o_ref, s_scr, c_scr, e_scr, ec_scr)
    edge = jnp.logical_or(j == 0, j == nj - 1)

    @pl.when(edge)
    def _():
        ty = jnp.where(j == 0, 0, 2)
        full = [(0, KPAIRS)] * QROWS
        _nat_tile(full, lambda qr, kp: tbl[(ty * QROWS + qr) * KPAIRS + kp], *args)

    @pl.when(jnp.logical_not(edge))
    def _():
        win = []
        for qr in range(QROWS):
            live = [kp for kp in range(KPAIRS) if _NAT_TBL[1, qr, kp] != N_BIAS_BLK - 1]
            win.append((live[0], live[-1] + 1))
        _nat_tile(win, lambda qr, kp: int(_NAT_TBL[1, qr, kp]), *args)


def _nat_attn(li, pn, ck, cv, pb, n_ctx_rows, n_lat):
    nj = TLAT // TQ
    qb0 = n_ctx_rows // TQ
    sb0 = n_ctx_rows // TLAT
    npair = NH // 2
    grid_spec = pltpu.PrefetchScalarGridSpec(
        num_scalar_prefetch=1,
        grid=(n_lat, npair, nj),
        in_specs=[
            pl.BlockSpec((TQ, 2 * HD), lambda b, p, j, t: (qb0 + b * nj + j, p)),
            pl.BlockSpec((TLAT, 2 * HD), lambda b, p, j, t: (sb0 + b, npair + p)),
            pl.BlockSpec((TLAT, 2 * HD), lambda b, p, j, t: (sb0 + b, 2 * npair + p)),
            pl.BlockSpec((None, None, SEQ, 2 * HD), lambda b, p, j, t: (li, b, 0, p)),
            pl.BlockSpec((None, None, SEQ, 2 * HD), lambda b, p, j, t: (li, b, 0, p)),
            pl.BlockSpec((None, 2, N_BIAS_BLK, GRID_W, 2 * GRID_W), lambda b, p, j, t: (li, p, 0, 0, 0)),
        ],
        out_specs=pl.BlockSpec((TQ, 2 * HD), lambda b, p, j, t: (b * nj + j, p)),
        scratch_shapes=[
            pltpu.VMEM((TQ, BAND), F32), pltpu.VMEM((TQ, SEQ), F32),
            pltpu.VMEM((TQ, BAND), BF16), pltpu.VMEM((TQ, SEQ), BF16),
        ],
    )
    return pl.pallas_call(
        _nat_kernel,
        grid_spec=grid_spec,
        out_shape=jax.ShapeDtypeStruct((n_lat * TLAT, NW), BF16),
        compiler_params=_cparams(("arbitrary", "arbitrary", "arbitrary")),
    )(jnp.asarray(_NAT_TBL.reshape(-1)), pn, pn, pn, ck, cv, pb)


def _proj_out_kernel(mrow, isctx, cidx, lidx, hprev, hnext, pblk, nblk,
                     x_ref, mod_ref, pc_ref, pcp_ref, pcn_ref, cw_ref, of_ref, ob_ref, gr_ref,
                     gng_ref, hm_ref, yc_ref, yl_ref, w_ref, o_ref):
    del mrow, cidx, lidx, pblk, nblk
    t = pl.program_id(0)
    pc = pc_ref[...].astype(F32)
    u = pc[:, CONV_W:2 * CONV_W] * pc[:, 2 * CONV_W:3 * CONV_W]
    pp = pcp_ref[7:8, :].astype(F32)
    pn = pcn_ref[0:1, :].astype(F32)
    u_prev_edge = pp[:, CONV_W:2 * CONV_W] * pp[:, 2 * CONV_W:3 * CONV_W] * hprev[t].astype(F32)
    u_next_edge = pn[:, CONV_W:2 * CONV_W] * pn[:, 2 * CONV_W:3 * CONV_W] * hnext[t].astype(F32)
    rows = lax.broadcasted_iota(jnp.int32, (TM_OUT, CONV_W), 0)
    u_prev = jnp.where(rows == 0, u_prev_edge, pltpu.roll(u, 1, 0))
    u_next = jnp.where(rows == TM_OUT - 1, u_next_edge, pltpu.roll(u, TM_OUT - 1, 0))
    ctx = isctx[t] == 1
    in_seq = rows & (SEQ - 1)
    u_prev = jnp.where(jnp.logical_and(ctx, in_seq == 0), 0.0, u_prev)
    u_next = jnp.where(jnp.logical_and(ctx, in_seq == SEQ - 1), 0.0, u_next)
    y_conv = pc[:, 0:CONV_W] * (cw_ref[0:1, :] * u_prev + cw_ref[1:2, :] * u + cw_ref[2:3, :] * u_next)

    o = of_ref[...].astype(F32) + ob_ref[...].astype(F32)
    on = o * lax.rsqrt(_head_mean(o * o, hm_ref[...]) + EPS) * gng_ref[...]
    y_gla = on * _silu(gr_ref[...].astype(F32))

    y_nat = jnp.where(isctx[t] == 1, yc_ref[...], yl_ref[...])

    y = (_dot(y_conv.astype(BF16), w_ref[0:CONV_W, :])
         + _dot(y_gla.astype(BF16), w_ref[CONV_W:CONV_W + GVW, :])
         + _dot(y_nat, w_ref[CONV_W + GVW:D, :]))
    o_ref[...] = x_ref[...] + mod_ref[2:3, :] * y


def _proj_out_plan(n_ctx, n_lat):
    nb = TLAT // TM_OUT
    nct = n_ctx * SEQ // TM_OUT
    nt = nct + n_lat * nb
    mrow, isctx, cidx, lidx, hprev, hnext, pblk, nblk = ([] for _ in range(8))
    per = TM_OUT // 8
    for t in range(nt):
        ctx = t < nct
        tl = t - nct
        mrow.append(0 if ctx else 1 + tl // nb)
        isctx.append(1 if ctx else 0)
        cidx.append(min(t, nct - 1))
        lidx.append(max(tl, 0))
        hprev.append(0 if ctx or tl % nb == 0 else 1)
        hnext.append(0 if ctx or tl % nb == nb - 1 else 1)
        pblk.append(max(t * per - 1, 0))
        nblk.append(min((t + 1) * per, nt * per - 1))
    return [np.asarray(a, np.int32) for a in (mrow, isctx, cidx, lidx, hprev, hnext, pblk, nblk)]


def _proj_out(li, x, mod, pc, conv_w, o_f, o_b, pv, gng, hm, y_ctx, y_lat, w_out, plan):
    r = x.shape[0]
    nt = plan[0].shape[0]
    row = lambda t, *_: (t, 0)
    const = lambda t, *_: (0, 0)
    layer = lambda t, *_: (li, 0, 0)
    grid_spec = pltpu.PrefetchScalarGridSpec(
        num_scalar_prefetch=8,
        grid=(nt,),
        in_specs=[
            pl.BlockSpec((TM_OUT, D), row),
            pl.BlockSpec((None, None, 6, D), lambda t, m, *_: (li, m[t], 0, 0)),
            pl.BlockSpec((TM_OUT, PC_W), row),
            pl.BlockSpec((8, PC_W), lambda t, m, ic, ci, lidx, hp, hn, pb, nb_: (pb[t], 0)),
            pl.BlockSpec((8, PC_W), lambda t, m, ic, ci, lidx, hp, hn, pb, nb_: (nb_[t], 0)),
            pl.BlockSpec((None, 3, CONV_W), layer),
            pl.BlockSpec((TM_OUT, GVW), row),
            pl.BlockSpec((TM_OUT, GVW), row),
            pl.BlockSpec((TM_OUT, GVW), lambda t, *_: (t, 1)),
            pl.BlockSpec((None, 1, GVW), layer),
            pl.BlockSpec((GVW, GVW), const),
            pl.BlockSpec((TM_OUT, NW), lambda t, m, ic, ci, *_: (ci[t], 0)),
            pl.BlockSpec((TM_OUT, NW), lambda t, m, ic, ci, lidx, *_: (lidx[t], 0)),
            pl.BlockSpec((None, D, D), layer),
        ],
        out_specs=pl.BlockSpec((TM_OUT, D), row),
    )
    return pl.pallas_call(
        _proj_out_kernel,
        grid_spec=grid_spec,
        out_shape=jax.ShapeDtypeStruct((r, D), F32),
        compiler_params=_cparams(("arbitrary",)),
    )(*plan, x, mod, pc, pc, pc, conv_w, o_f, o_b, pv, gng, hm, y_ctx, y_lat, w_out)


def _ffn_dense_kernel(mrow, x_ref, g_ref, mod_ref, wg_ref, wu_ref, wd_ref, o_ref, h_scr, acc):
    del mrow
    f = pl.program_id(1)

    @pl.when(f == 0)
    def _():
        h_scr[...] = _norm_mod(x_ref[...], g_ref[...], mod_ref[3:4, :], mod_ref[4:5, :]).astype(BF16)
        acc[...] = jnp.zeros_like(acc)

    h = h_scr[...]
    hid = _silu(_dot(h, wg_ref[...])) * _dot(h, wu_ref[...])
    acc[...] += _dot(hid.astype(BF16), wd_ref[...])

    @pl.when(f == pl.num_programs(1) - 1)
    def _():
        o_ref[...] = x_ref[...] + mod_ref[5:6, :] * acc[...]


def _ffn_dense(li, j, tm, tf, x, g2, mod, wg, wu, wd, mrow):
    r = x.shape[0]
    nf = D_FF // tf
    grid_spec = pltpu.PrefetchScalarGridSpec(
        num_scalar_prefetch=1,
        grid=(r // tm, nf),
        in_specs=[
            pl.BlockSpec((tm, D), lambda t, f, m: (t, 0)),
            pl.BlockSpec((None, 1, D), lambda t, f, m: (li, 0, 0)),
            pl.BlockSpec((None, None, 6, D), lambda t, f, m: (li, m[t], 0, 0)),
            pl.BlockSpec((None, D, tf), lambda t, f, m: (j, 0, f)),
            pl.BlockSpec((None, D, tf), lambda t, f, m: (j, 0, f)),
            pl.BlockSpec((None, tf, D), lambda t, f, m: (j, f, 0)),
        ],
        out_specs=pl.BlockSpec((tm, D), lambda t, f, m: (t, 0)),
        scratch_shapes=[pltpu.VMEM((tm, D), BF16), pltpu.VMEM((tm, D), F32)],
    )
    return pl.pallas_call(
        _ffn_dense_kernel,
        grid_spec=grid_spec,
        out_shape=jax.ShapeDtypeStruct((r, D), F32),
        compiler_params=_cparams(("arbitrary", "arbitrary")),
    )(mrow, x, g2, mod, wg, wu, wd)


DH = D // 2
HI_MASK = 0xFFFF0000


def _pack_rows(a):
    u = pltpu.bitcast(a.astype(BF16).astype(F32), jnp.uint32)
    return (u[:, 0:DH] >> 16) | u[:, DH:D]


def _unpack_rows(u):
    lo = pltpu.bitcast(u << 16, F32)
    hi = pltpu.bitcast(u & jnp.uint32(HI_MASK), F32)
    return lo, hi


def _router_kernel(mrow, x_ref, g_ref, mod_ref, rhi_ref, rlo_ref, h_ref, ri_ref, rw_ref):
    del mrow
    h = _norm_mod(x_ref[...], g_ref[...], mod_ref[3:4, :], mod_ref[4:5, :])
    h_ref[...] = _pack_rows(h)
    hhi, hlo = _split_bf16(h)
    logits = _dot(hhi, rhi_ref[...]) + _dot(hlo, rhi_ref[...]) + _dot(hhi, rlo_ref[...])
    lane = lax.broadcasted_iota(jnp.int32, logits.shape, 1).astype(F32)
    lg = jnp.where(lane < N_EXP, logits, -jnp.inf)
    m1 = lg.max(axis=-1, keepdims=True)
    i1 = jnp.where(lg == m1, lane, 128.0).min(axis=-1, keepdims=True)
    lg2 = jnp.where(lane == i1, -jnp.inf, lg)
    m2 = lg2.max(axis=-1, keepdims=True)
    i2 = jnp.where(lg2 == m2, lane, 128.0).min(axis=-1, keepdims=True)
    e = jnp.exp(m2 - m1)
    w1 = 1.0 / (1.0 + e)
    w2 = e / (1.0 + e)
    ri_ref[...] = jnp.where(lane == 0.0, i1, jnp.where(lane == 1.0, i2, 0.0)).astype(jnp.int32)
    rw_ref[...] = jnp.where(lane == 0.0, w1, jnp.where(lane == 1.0, w2, 0.0))


def _router(li, j, x, g2, mod, rhi, rlo, mrow):
    r = x.shape[0]
    row = lambda t, m: (t, 0)
    grid_spec = pltpu.PrefetchScalarGridSpec(
        num_scalar_prefetch=1,
        grid=(r // TM_PROJ,),
        in_specs=[
            pl.BlockSpec((TM_PROJ, D), row),
            pl.BlockSpec((None, 1, D), lambda t, m: (li, 0, 0)),
            pl.BlockSpec((None, None, 6, D), lambda t, m: (li, m[t], 0, 0)),
            pl.BlockSpec((None, D, 128), lambda t, m: (j, 0, 0)),
            pl.BlockSpec((None, D, 128), lambda t, m: (j, 0, 0)),
        ],
        out_specs=[
            pl.BlockSpec((TM_PROJ, DH), row),
            pl.BlockSpec((TM_PROJ, 128), row),
            pl.BlockSpec((TM_PROJ, 128), row),
        ],
    )
    return pl.pallas_call(
        _router_kernel,
        grid_spec=grid_spec,
        out_shape=[
            jax.ShapeDtypeStruct((r, DH), jnp.uint32),
            jax.ShapeDtypeStruct((r, 128), jnp.int32),
            jax.ShapeDtypeStruct((r, 128), F32),
        ],
        compiler_params=_cparams(("arbitrary",)),
    )(mrow, x, g2, mod, rhi, rlo)


def _row_copy(src_ref, src_row, dst_ref, dst_row, sem):
    return pltpu.make_async_copy(src_ref.at[pl.ds(src_row, 1)], dst_ref.at[pl.ds(dst_row, 1)], sem)


def _dispatch_kernel(dest, pstart, plen, nused, h_ref, xg_hbm, zbuf, sem, zsem):
    t = pl.program_id(0)
    base = t * 2 * TD_DISP
    tm = zbuf.shape[0]
    n_tiles = xg_hbm.shape[0] // tm

    def start(rr, c):
        _row_copy(h_ref, rr, xg_hbm, dest[base + 2 * rr], sem).start(priority=0)
        _row_copy(h_ref, rr, xg_hbm, dest[base + 2 * rr + 1], sem).start(priority=1)
        return c

    lax.fori_loop(0, TD_DISP, start, 0, unroll=8)

    @pl.when(t == 0)
    def _():
        zbuf[...] = jnp.zeros_like(zbuf)
        for e in range(N_EXP):
            n = plen[e]
            s0 = pstart[e]
            head = n & 7
            pads = [(rr < head, _row_copy(zbuf, 0, xg_hbm, s0 + rr, zsem)) for rr in range(7)]
            for bit in reversed(range(3, tm.bit_length() - 1)):
                size = 1 << bit
                first = pl.multiple_of(s0 + head + ((n - head) & ~(2 * size - 1)), 8)
                pads.append(((n & size) != 0,
                             pltpu.make_async_copy(zbuf.at[pl.ds(0, size)], xg_hbm.at[pl.ds(first, size)], zsem)))
            for take, cp in pads:
                pl.when(take)(cp.start)
            for take, cp in pads:
                pl.when(take)(cp.wait)

        def tile_copy(i):
            return pltpu.make_async_copy(zbuf, xg_hbm.at[pl.ds(i * tm, tm)], zsem)

        def tstart(i, c):
            tile_copy(i).start()
            return c

        def twait(i, c):
            tile_copy(i).wait()
            return c

        lax.fori_loop(nused[0], n_tiles, tstart, 0)
        lax.fori_loop(nused[0], n_tiles, twait, 0)

    def wait(rr, c):
        _row_copy(h_ref, 0, xg_hbm, 0, sem).wait()
        return c

    lax.fori_loop(0, 2 * TD_DISP, wait, 0, unroll=8)


def _dispatch(tm, h, dest, pstart, plen, nused, n_tiles):
    r = h.shape[0]
    grid_spec = pltpu.PrefetchScalarGridSpec(
        num_scalar_prefetch=4,
        grid=(r // TD_DISP,),
        in_specs=[pl.BlockSpec((TD_DISP, DH), lambda t, *_: (t, 0))],
        out_specs=pl.BlockSpec(memory_space=pl.ANY),
        scratch_shapes=[pltpu.VMEM((tm, DH), jnp.uint32), pltpu.SemaphoreType.DMA(()),
                        pltpu.SemaphoreType.DMA(())],
    )
    return pl.pallas_call(
        _dispatch_kernel,
        grid_spec=grid_spec,
        out_shape=jax.ShapeDtypeStruct((n_tiles * tm, DH), jnp.uint32),
        compiler_params=_cparams(("arbitrary",)),
    )(dest, pstart, plen, nused, h)


def _ffn_grouped_kernel(te, nused, x_ref, wg_ref, wu_ref, wd_ref, o_ref, xb, acc):
    del te
    i = pl.program_id(0)
    f = pl.program_id(1)
    last = pl.num_programs(1) - 1
    used = i < nused[0]

    @pl.when(jnp.logical_and(used, f == 0))
    def _():
        lo, hi = _unpack_rows(x_ref[...])
        xb[:, 0:DH] = lo.astype(BF16)
        xb[:, DH:D] = hi.astype(BF16)
        acc[...] = jnp.zeros_like(acc)

    @pl.when(used)
    def _():
        h = xb[...]
        hid = _silu(_dot(h, wg_ref[...].astype(BF16))) * _dot(h, wu_ref[...].astype(BF16))
        acc[...] += _dot(hid.astype(BF16), wd_ref[...].astype(BF16))

    @pl.when(jnp.logical_and(used, f == last))
    def _():
        o_ref[...] = _pack_rows(acc[...])

    @pl.when(jnp.logical_and(jnp.logical_not(used), f == last))
    def _():
        o_ref[...] = jnp.zeros_like(o_ref)


def _ffn_grouped(j, tm, tf, xg, wg, wu, wd, te, nused, n_tiles):
    nf = D_EXP // tf
    fidx = lambda i, f, n: jnp.where(i < n[0], f, nf - 1)
    grid_spec = pltpu.PrefetchScalarGridSpec(
        num_scalar_prefetch=2,
        grid=(n_tiles, nf),
        in_specs=[
            pl.BlockSpec((tm, DH), lambda i, f, e, n: (jnp.minimum(i, n[0] - 1), 0)),
            pl.BlockSpec((None, None, D, tf), lambda i, f, e, n: (j, e[i], 0, fidx(i, f, n))),
            pl.BlockSpec((None, None, D, tf), lambda i, f, e, n: (j, e[i], 0, fidx(i, f, n))),
            pl.BlockSpec((None, None, tf, D), lambda i, f, e, n: (j, e[i], fidx(i, f, n), 0)),
        ],
        out_specs=pl.BlockSpec((tm, DH), lambda i, f, e, n: (i, 0)),
        scratch_shapes=[pltpu.VMEM((tm, D), BF16), pltpu.VMEM((tm, D), F32)],
    )
    return pl.pallas_call(
        _ffn_grouped_kernel,
        grid_spec=grid_spec,
        out_shape=jax.ShapeDtypeStruct((n_tiles * tm, DH), jnp.uint32),
        compiler_params=_cparams(("arbitrary", "arbitrary")),
    )(te, nused, xg, wg, wu, wd)


def _combine_kernel(mrow, dest, x_ref, mod_ref, rw_ref, y_hbm, o_ref, buf, sem):
    del mrow
    t = pl.program_id(0)
    base = t * TC_COMB * 2

    def start(rr, c):
        _row_copy(y_hbm, dest[base + 2 * rr], buf, 2 * rr, sem).start(priority=0)
        _row_copy(y_hbm, dest[base + 2 * rr + 1], buf, 2 * rr + 1, sem).start(priority=1)
        return c

    lax.fori_loop(0, TC_COMB, start, 0, unroll=8)

    def wait(rr, c):
        _row_copy(y_hbm, 0, buf, 0, sem).wait()
        return c

    lax.fori_loop(0, 2 * TC_COMB, wait, 0, unroll=8)
    rw = rw_ref[...]
    lo0, hi0 = _unpack_rows(buf[0:TC_COMB, :])
    lo1, hi1 = _unpack_rows(buf[TC_COMB:2 * TC_COMB, :])
    w0, w1 = rw[:, 0:1], rw[:, 1:2]
    o_ref[:, 0:DH] = x_ref[:, 0:DH] + mod_ref[5:6, 0:DH] * (w0 * lo0 + w1 * lo1)
    o_ref[:, DH:D] = x_ref[:, DH:D] + mod_ref[5:6, DH:D] * (w0 * hi0 + w1 * hi1)


def _combine(li, x, mod, rw, yg, dest, mrow):
    r = x.shape[0]
    grid_spec = pltpu.PrefetchScalarGridSpec(
        num_scalar_prefetch=2,
        grid=(r // TC_COMB,),
        in_specs=[
            pl.BlockSpec((TC_COMB, D), lambda t, m, d: (t, 0)),
            pl.BlockSpec((None, None, 6, D), lambda t, m, d: (li, m[t], 0, 0)),
            pl.BlockSpec((TC_COMB, 128), lambda t, m, d: (t, 0)),
            pl.BlockSpec(memory_space=pl.ANY),
        ],
        out_specs=pl.BlockSpec((TC_COMB, D), lambda t, m, d: (t, 0)),
        scratch_shapes=[pltpu.VMEM((2 * TC_COMB, DH), jnp.uint32), pltpu.SemaphoreType.DMA(())],
    )
    return pl.pallas_call(
        _combine_kernel,
        grid_spec=grid_spec,
        out_shape=jax.ShapeDtypeStruct((r, D), F32),
        compiler_params=_cparams(("arbitrary",)),
    )(mrow, dest, x, mod, rw, yg)


def _moe_plan(ri, tm, n_tiles):
    r = ri.shape[0]
    ef = ri[:, :2].reshape(-1)
    oh = (ef[:, None] == jnp.arange(N_EXP, dtype=jnp.int32)[None, :]).astype(jnp.int32)
    csum = jnp.cumsum(oh, axis=0)
    pos = jnp.sum(csum * oh, axis=1) - 1
    counts = csum[-1]
    tiles = (counts + tm - 1) // tm
    tile_end = jnp.cumsum(tiles)
    off = (tile_end - tiles) * tm
    dest = (jnp.sum(off[None, :] * oh, axis=1) + pos).astype(jnp.int32)
    tile_id = jnp.arange(n_tiles, dtype=jnp.int32)
    te = jnp.minimum(jnp.sum((tile_end[None, :] <= tile_id[:, None]).astype(jnp.int32), axis=1), N_EXP - 1)
    nused = tile_end[-1:].astype(jnp.int32)
    pstart = (off + counts).astype(jnp.int32)
    plen = (tiles * tm - counts).astype(jnp.int32)
    dest_t = dest.reshape(r // TC_COMB, TC_COMB, 2).transpose(0, 2, 1).reshape(-1)
    return dest, dest_t, te.astype(jnp.int32), nused, pstart, plen


def _mod_rows(n_ctx, n_lat, tile):
    rows = [0] * (n_ctx * SEQ // tile)
    for b in range(n_lat):
        rows += [1 + b] * (TLAT // tile)
    return np.asarray(rows, np.int32)


def _rope_tables():
    t = np.arange(TLAT)
    lane = np.arange(GWP)
    p = lane % GK
    sub = p % (GK // 2)
    nf = GK // 4
    freq = ROPE_BASE ** (-(sub % nf).astype(np.float32) / nf)
    pos = np.where((p < GK // 2)[None, :], (t // GRID_W)[:, None], (t % GRID_W)[:, None]).astype(np.float32)
    ang = jnp.asarray(pos) * jnp.asarray(freq.astype(np.float32))[None, :]
    cos, sin = jnp.cos(ang), jnp.sin(ang)
    lowhalf = jnp.asarray((sub < nf)[None, :])
    sina = jnp.where(lowhalf, -sin, 0.0)
    sinb = jnp.where(lowhalf, 0.0, sin)
    pad1 = jnp.ones((TM_PROJ, GWP), F32)
    pad0 = jnp.zeros((TM_PROJ, GWP), F32)
    return (jnp.concatenate([cos, pad1]), jnp.concatenate([sina, pad0]), jnp.concatenate([sinb, pad0]))


def _nat_bias_blocks(rpb):
    nl = rpb.shape[0]
    col = np.arange(GRID_W)
    c0 = np.clip(col - NAT_KW // 2, 0, GRID_W - NAT_KW)
    in_win = (col[None, :] >= c0[:, None]) & (col[None, :] < c0[:, None] + NAT_KW)
    dc = np.clip(col[None, :] - col[:, None], -(NAT_KW - 1), NAT_KW - 1) + NAT_KW - 1
    onehot = np.zeros((2 * NAT_KW - 1, GRID_W, GRID_W), np.float32)
    onehot[dc, col[:, None], col[None, :]] = 1.0
    sel = jnp.einsum("lhad,dqk->lhaqk", rpb, jnp.asarray(onehot), precision=lax.Precision.HIGHEST)
    cm = jnp.where(jnp.asarray(in_win)[None, None, None], sel, NEG)
    na = 2 * NAT_KH - 1
    neg = jnp.full((nl, NH, NAT_KH, GRID_W, GRID_W), NEG, F32)
    full = jnp.concatenate([cm[:, :, 0:na - 1], cm[:, :, 1:na]], axis=-1)
    left = jnp.concatenate([neg, cm[:, :, 0:NAT_KH]], axis=-1)
    right = jnp.concatenate([cm[:, :, NAT_KH - 1:na], neg], axis=-1)
    none = jnp.concatenate([neg[:, :, 0:1], neg[:, :, 0:1]], axis=-1)
    return jnp.concatenate([full, left, right, none], axis=2)


def _heads_to_rows(a):
    b, h, t, d = a.shape
    return a.transpose(0, 2, 1, 3).reshape(b, t, h * d)


def _rows_to_heads(a, b, t):
    return a.reshape(b, t, NH, HD).transpose(0, 2, 1, 3)


def _state_to_blockdiag(s):
    b = s.shape[0]
    eye = jnp.eye(GH, dtype=s.dtype)
    bd = jnp.einsum("bhkv,hg->bhvgk", s, eye).reshape(b, GVW, GW)
    return jnp.pad(bd, ((0, 0), (0, 0), (0, GWP - GW)))


def _blockdiag_to_state(st):
    b = st.shape[0]
    s5 = st[:, :, :GW].reshape(b, GH, GV, GH, GK)
    return jnp.stack([s5[:, h, :, h, :] for h in range(GH)], axis=1).transpose(0, 1, 3, 2)


def kernel(x_prompt, x_sample, cache_nat_k, cache_nat_v, state_gla, c, c_ctx, norm1_g, norm2_g, w_mod, b_mod, w_in, conv_w, gla_gate_w2, gla_gate_b, gla_norm_g, nat_q_norm_g, nat_k_norm_g, nat_rpb, w_out, ffn_w_gate, ffn_w_up, ffn_w_down, moe_router, moe_w_gate, moe_w_up, moe_w_down):
    n_ctx, n_lat = x_prompt.shape[0], x_sample.shape[0]
    depth = w_in.shape[0]
    ncr = n_ctx * SEQ
    r = ncr + n_lat * TLAT
    assert ncr % TLAT == 0 and n_lat + 1 <= 8

    x = jnp.concatenate([x_prompt.reshape(ncr, D), x_sample.reshape(n_lat * TLAT, D)], axis=0)
    cvecs = jnp.zeros((8, D), F32).at[0].set(c_ctx).at[1:1 + n_lat].set(c)
    mod = _modulation(cvecs, w_mod, b_mod)

    z = lambda n: jnp.zeros((depth, D, n), F32)
    w_in_p = jnp.concatenate([
        w_in[:, :, 768:960], z(GWP - GW), w_in[:, :, 960:1152], z(GWP - GW),
        w_in[:, :, 1920:1952], z(128 - 2 * LR),
        w_in[:, :, 0:768], w_in[:, :, 1152:1920], w_in[:, :, 1952:3104]], axis=-1).astype(BF16)
    w_out_b = w_out.astype(BF16)
    w2 = jnp.pad(gla_gate_w2, ((0, 0), (0, 0), (0, 0), (0, GWP - GW)))
    w2f = jnp.pad(w2[:, 0], ((0, 0), (0, 128 - LR), (0, 0))).astype(BF16)
    w2b = jnp.pad(w2[:, 1], ((0, 0), (LR, 128 - 2 * LR), (0, 0))).astype(BF16)
    gb = jnp.pad(gla_gate_b, ((0, 0), (0, 0), (0, GWP - GW)))[:, :, None, :]
    hm = jnp.asarray(np.kron(np.eye(NH), np.full((HD, HD), 1.0 / HD)), BF16)
    g1 = norm1_g[:, None, :]
    g2 = norm2_g[:, None, :]
    qg = jnp.tile(nat_q_norm_g, (1, NH))[:, None, :]
    kg = jnp.tile(nat_k_norm_g, (1, NH))[:, None, :]
    gng = jnp.tile(gla_norm_g, (1, GH))[:, None, :]
    gla_masks = _gla_masks()
    rope = _rope_tables()
    pb = _nat_bias_blocks(nat_rpb)
    ck = _heads_to_rows(cache_nat_k.transpose(1, 0, 2, 3, 4).reshape(depth * n_lat, NH, SEQ, HD))
    cv = _heads_to_rows(cache_nat_v.transpose(1, 0, 2, 3, 4).reshape(depth * n_lat, NH, SEQ, HD))
    ck = ck.reshape(depth, n_lat, SEQ, NW).astype(BF16)
    cv = cv.reshape(depth, n_lat, SEQ, NW).astype(BF16)
    ffn_g, ffn_u, ffn_d = ffn_w_gate.astype(BF16), ffn_w_up.astype(BF16), ffn_w_down.astype(BF16)
    router_p = jnp.pad(moe_router, ((0, 0), (0, 0), (0, 128 - N_EXP)))
    rhi = router_p.astype(BF16)
    rlo = (router_p - rhi.astype(F32)).astype(BF16)

    mrow_proj = jnp.asarray(_mod_rows(n_ctx, n_lat, TM_PROJ))
    in_plan = [jnp.asarray(a) for a in _proj_in_plan(n_ctx, n_lat)]
    mrow_comb = jnp.asarray(_mod_rows(n_ctx, n_lat, TC_COMB))
    gla_plan = [jnp.asarray(a) for a in _gla_plan(n_ctx, n_lat)]
    out_plan = [jnp.asarray(a) for a in _proj_out_plan(n_ctx, n_lat)]

    new_k, new_v, new_s = [], [], []
    for i in range(depth):
        pg, pc, pv, pn, knf, vnf = _proj_in(i, x, g1, mod, w_in_p, hm, qg, kg, rope, w2f, w2b, gb,
                                            in_plan, ncr)
        s0f = _state_to_blockdiag(state_gla[:, i, 0])
        s0b = _state_to_blockdiag(state_gla[:, i, 1])
        o_f, o_b, sff, sfb = _gla(pg, pv, gla_masks, s0f, s0b, gla_plan, n_ctx)
        y_ctx = _ctx_attn(pn, n_ctx)
        y_lat = _nat_attn(i, pn, ck, cv, pb, ncr, n_lat)
        x = _proj_out(i, x, mod, pc, conv_w, o_f, o_b, pv, gng, hm, y_ctx, y_lat, w_out_b, out_plan)
        j = i // 2
        if i % 2 == 0:
            tm, tf = FFN_TILE
            mrow_ffn = jnp.asarray(_mod_rows(n_ctx, n_lat, tm))
            x = _ffn_dense(i, j, tm, tf, x, g2, mod, ffn_g, ffn_u, ffn_d, mrow_ffn)
        else:
            tm, tf = MOE_TILE
            n_tiles = 2 * r // tm + N_EXP
            h, ri, rw = _router(i, j, x, g2, mod, rhi, rlo, mrow_proj)
            dest, dest_t, te, nused, pstart, plen = _moe_plan(ri, tm, n_tiles)
            xg = _dispatch(tm, h, dest, pstart, plen, nused, n_tiles)
            yg = _ffn_grouped(j, tm, tf, xg, moe_w_gate, moe_w_up, moe_w_down, te, nused, n_tiles)
            x = _combine(i, x, mod, rw, yg, dest_t, mrow_comb)
        new_k.append(_rows_to_heads(knf, n_ctx, SEQ))
        new_v.append(_rows_to_heads(vnf, n_ctx, SEQ))
        new_s.append(jnp.stack([_blockdiag_to_state(sff), _blockdiag_to_state(sfb)], axis=1))

    y_prompt = x[:ncr].reshape(n_ctx, SEQ, D)
    y_sample = x[ncr:].reshape(n_lat, TLAT, D)
    return (y_prompt, y_sample, jnp.stack(new_k, axis=1), jnp.stack(new_v, axis=1), jnp.stack(new_s, axis=1))
```
